```python
import math
import jax, jax.numpy as jnp
from jax import lax
import numpy as np

D_MODEL = 1024
BATCH = 8
SEQ = 8192
DEPTH = 1

CTX_LEN = 256
GRID_W = 64
MIX_W = D_MODEL
F_W = MIX_W // 2
F_GROUPS = 4
F_GROUP_W = F_W // F_GROUPS
DN_W = MIX_W - F_W
DN_HEADS = 4
DK = DN_W // DN_HEADS
DV = DN_W // DN_HEADS
QK_W = DN_HEADS * DK
V_W = DN_HEADS * DV
CONV_CH = 2 * QK_W + V_W
W_IN = F_W + CONV_CH + V_W + 4 * DN_HEADS
CONV_K = 3
CHUNK = 64
N_GROUPS = 4
EXPERTS_PER_GROUP = 8
N_EXPERTS = N_GROUPS * EXPERTS_PER_GROUP
TOP_K = 2
D_EXPERT = D_MODEL // 2
MOE_BLOCK = 128
EPS = 1e-6

kernel_name = "hymba_fnet_gdn_hmoe_dit"


def rmsnorm(x, g):
    xf = x.astype(jnp.float32)
    y = xf * lax.rsqrt(jnp.mean(xf * xf, axis=-1, keepdims=True) + EPS) * g.astype(jnp.float32)
    return y.astype(x.dtype)


def l2norm(a):
    af = a.astype(jnp.float32)
    return af * lax.rsqrt(jnp.sum(af * af, axis=-1, keepdims=True) + EPS)


def modulate(h, shift, scale):
    return h * (1 + scale) + shift


def conv_grid(u, w):
    b, t, ch = u.shape
    rows = t // GRID_W
    y = lax.conv_general_dilated(u.reshape(b, rows, GRID_W, ch), w[:, :, None, :].astype(u.dtype),
                                 (1, 1), 'SAME', dimension_numbers=('NHWC', 'HWIO', 'NHWC'),
                                 feature_group_count=ch)
    return y.reshape(b, t, ch)


def conv_seq(u, w):
    ch = u.shape[-1]
    return lax.conv_general_dilated(u, w[1][:, None, :].astype(u.dtype), (1,), 'SAME',
                                    dimension_numbers=('NWC', 'WIO', 'NWC'), feature_group_count=ch)


def fourier_mix(f):
    b, t, _ = f.shape
    fg = f.astype(jnp.float32).reshape(b, t, F_GROUPS, F_GROUP_W)
    out = jnp.fft.fft2(fg, axes=(1, 3), norm='ortho').real
    return out.reshape(b, t, F_W).astype(f.dtype)


def gated_delta_chunked(q, k, v, g, beta, s0):
    b, t, h, dk = q.shape
    dv = v.shape[-1]
    n = t // CHUNK

    def chunks(a):
        a = jnp.moveaxis(a.astype(jnp.float32), 2, 1)
        return a.reshape(b, h, n, CHUNK, *a.shape[3:])

    q = chunks(q) * dk ** -0.5
    k = chunks(k)
    v = chunks(v)
    beta = chunks(beta)
    gc = jnp.cumsum(chunks(g), axis=-1)
    idx = jnp.arange(CHUNK)
    lower = idx[:, None] >= idx[None, :]
    decay = jnp.exp(jnp.where(lower, gc[..., :, None] - gc[..., None, :], -jnp.inf))
    k_beta = k * beta[..., None]
    m = jnp.where(idx[:, None] > idx[None, :],
                  jnp.einsum('bhnid,bhnjd->bhnij', k_beta, k) * decay, 0.0)
    eye = jnp.eye(CHUNK, dtype=jnp.float32)
    a_mat = eye + m
    t_inv = lax.linalg.triangular_solve(a_mat, jnp.broadcast_to(eye, a_mat.shape), left_side=True,
                                        lower=True, unit_diagonal=True)
    u = t_inv @ (v * beta[..., None])
    w = t_inv @ (k_beta * jnp.exp(gc)[..., None])
    intra = jnp.einsum('bhnid,bhnjd->bhnij', q, k) * decay

    def step(s, blk):
        q_i, k_i, u_i, w_i, a_i, g_i = blk
        v_new = u_i - w_i @ s
        o_i = (q_i * jnp.exp(g_i)[..., None]) @ s + a_i @ v_new
        g_last = g_i[..., -1:]
        s = s * jnp.exp(g_last)[..., None] + jnp.einsum(
            'bhcd,bhce->bhde', k_i * jnp.exp(g_last - g_i)[..., None], v_new)
        return s, o_i

    xs = tuple(jnp.moveaxis(a, 2, 0) for a in (q, k, u, w, intra, gc))
    s_final, o = lax.scan(step, s0.astype(jnp.float32), xs)
    o = jnp.moveaxis(o, 0, 2).reshape(b, h, t, dv)
    return jnp.moveaxis(o, 1, 2), s_final


def bidir_delta(q, k, v, g, beta, s0_fwd, s0_bwd):
    flip = lambda a: jnp.flip(a, axis=1)
    o_f, s_f = gated_delta_chunked(q, k, v, g[:, :, 0], beta[:, :, 0], s0_fwd)
    o_b, s_b = gated_delta_chunked(flip(q), flip(k), flip(v), flip(g[:, :, 1]), flip(beta[:, :, 1]), s0_bwd)
    return o_f + flip(o_b), s_f, s_b


def mixer_features(h, w_in, conv_w, a_log, dt_bias, conv_fn):
    b, t = h.shape[:2]
    p = h @ w_in
    f, qkv, z, gates = jnp.split(p, [F_W, F_W + CONV_CH, F_W + CONV_CH + V_W], axis=-1)
    qkv = jax.nn.silu(conv_fn(qkv, conv_w))
    q, k, v = jnp.split(qkv, [QK_W, 2 * QK_W], axis=-1)
    q = l2norm(q.reshape(b, t, DN_HEADS, DK))
    k = l2norm(k.reshape(b, t, DN_HEADS, DK))
    v = v.reshape(b, t, DN_HEADS, DV)
    gates = gates.astype(jnp.float32).reshape(b, t, 2, 2, DN_HEADS)
    beta = jax.nn.sigmoid(gates[:, :, 0])
    g = -jnp.exp(a_log.astype(jnp.float32)) * jax.nn.softplus(gates[:, :, 1] + dt_bias.astype(jnp.float32))
    return f, q, k, v, z, g, beta


def mixer_output(f, o, z, onorm_g, w_out):
    b, t = f.shape[:2]
    o = (rmsnorm(o, onorm_g) * jax.nn.silu(z.reshape(b, t, DN_HEADS, DV))).astype(f.dtype)
    return jnp.concatenate([fourier_mix(f), o.reshape(b, t, V_W)], axis=-1) @ w_out


def moe_ffn(h, w_group, b_group, w_router, b_router, w_gate, w_up, w_down):
    shp = h.shape
    d = shp[-1]
    xf = h.reshape(-1, d)
    n = xf.shape[0]
    pg = jax.nn.softmax((xf @ w_group).astype(jnp.float32) + b_group.astype(jnp.float32), axis=-1)
    pg_top, g_top = lax.top_k(pg, 1)
    el = ((xf @ w_router).astype(jnp.float32) + b_router.astype(jnp.float32)).reshape(
        n, N_GROUPS, EXPERTS_PER_GROUP)
    el_g = el[jnp.arange(n), g_top[:, 0]]
    pe_top, e_top = lax.top_k(jax.nn.softmax(el_g, axis=-1), TOP_K)
    wts = pg_top * pe_top / jnp.sum(pe_top, axis=-1, keepdims=True)
    eid = (g_top * EXPERTS_PER_GROUP + e_top).reshape(-1).astype(jnp.int32)
    w_f = wts.reshape(-1)
    n_assign = n * TOP_K
    order = jnp.argsort(eid)
    e_s = eid[order]
    tok_s = (order // TOP_K).astype(jnp.int32)
    w_s = w_f[order]
    counts = jnp.zeros((N_EXPERTS,), jnp.int32).at[eid].add(1)
    start = jnp.cumsum(counts) - counts
    pcounts = (counts + MOE_BLOCK - 1) // MOE_BLOCK * MOE_BLOCK
    pend = jnp.cumsum(pcounts)
    pstart = pend - pcounts
    dest = pstart[e_s] + (jnp.arange(n_assign, dtype=jnp.int32) - start[e_s])
    n_slots = -(-n_assign // MOE_BLOCK) * MOE_BLOCK + N_EXPERTS * MOE_BLOCK
    n_blocks = n_slots // MOE_BLOCK
    slot_tok = jnp.full((n_slots,), n, jnp.int32).at[dest].set(tok_s)
    slot_w = jnp.zeros((n_slots,), jnp.float32).at[dest].set(w_s)
    blk_expert = jnp.minimum(
        jnp.searchsorted(pend, jnp.arange(n_blocks, dtype=jnp.int32) * MOE_BLOCK, side='right'),
        N_EXPERTS - 1)
    x_pad = jnp.concatenate([xf, jnp.zeros((1, d), xf.dtype)], axis=0)
    xs = x_pad[slot_tok].reshape(n_blocks, MOE_BLOCK, d)

    def expert_block(args):
        xb, e = args
        hid = jax.nn.silu(xb @ w_gate[e]) * (xb @ w_up[e])
        return hid @ w_down[e]

    ys = lax.map(expert_block, (xs, blk_expert)).reshape(n_slots, d)
    out = jnp.zeros((n + 1, d), ys.dtype).at[slot_tok].add(ys * slot_w[:, None].astype(ys.dtype))
    return out[:n].reshape(shp)


def trunk_layer(x, ctx, c, c_ctx, w_mod, b_mod, norm1_g, w_in, conv_w, a_log, dt_bias, onorm_g, w_out,
                norm2_g, w_group, b_group, w_router, b_router, w_gate, w_up, w_down, update_ctx):
    mod_x = jnp.split((jax.nn.silu(c) @ w_mod + b_mod)[:, None, :], 6, axis=-1)
    mod_c = jnp.split((jax.nn.silu(c_ctx) @ w_mod + b_mod)[None, None, :], 6, axis=-1)

    hc = modulate(rmsnorm(ctx, norm1_g), mod_c[0], mod_c[1])
    fc, qc, kc, vc, zc, gc, bc = mixer_features(hc, w_in, conv_w, a_log, dt_bias, conv_seq)
    zero_state = jnp.zeros((ctx.shape[0], DN_HEADS, DK, DV), jnp.float32)
    oc, s_fwd, s_bwd = bidir_delta(qc, kc, vc, gc, bc, zero_state, zero_state)

    hx = modulate(rmsnorm(x, norm1_g), mod_x[0], mod_x[1])
    fx, qx, kx, vx, zx, gx, bx = mixer_features(hx, w_in, conv_w, a_log, dt_bias, conv_grid)
    ox, _, _ = bidir_delta(qx, kx, vx, gx, bx, s_fwd, s_bwd)
    x = x + mod_x[2] * mixer_output(fx, ox, zx, onorm_g, w_out)
    x = x + mod_x[5] * moe_ffn(modulate(rmsnorm(x, norm2_g), mod_x[3], mod_x[4]),
                               w_group, b_group, w_router, b_router, w_gate, w_up, w_down)
    if update_ctx:
        ctx = ctx + mod_c[2] * mixer_output(fc, oc, zc, onorm_g, w_out)
        ctx = ctx + mod_c[5] * moe_ffn(modulate(rmsnorm(ctx, norm2_g), mod_c[3], mod_c[4]),
                                       w_group, b_group, w_router, b_router, w_gate, w_up, w_down)
    return x, ctx


def setup_inputs(seed: int = 0) -> dict:
    key = jax.random.key(seed)
    ks = jax.random.split(key, 24)
    nrm = lambda k, shape, s: jax.random.normal(k, shape, jnp.float32) * s
    x = nrm(ks[0], (BATCH, SEQ, D_MODEL), 1.0)
    c = nrm(ks[1], (BATCH, D_MODEL), 1.0)
    ctx = nrm(ks[2], (BATCH, CTX_LEN, D_MODEL), 1.0)
    c_ctx = nrm(ks[3], (D_MODEL,), 1.0)
    w_mod = nrm(ks[4], (DEPTH, D_MODEL, 6 * D_MODEL), 0.5 * D_MODEL ** -0.5)
    b_mod = nrm(ks[5], (DEPTH, 6 * D_MODEL), 0.01)
    norm1_g = 1.0 + nrm(ks[6], (DEPTH, D_MODEL), 0.02)
    w_in = nrm(ks[7], (DEPTH, D_MODEL, W_IN), D_MODEL ** -0.5)
    conv_w = nrm(ks[8], (DEPTH, CONV_K, CONV_K, CONV_CH), 1.0 / CONV_K)
    a_log = jnp.log(jax.random.uniform(ks[9], (DEPTH, 2, DN_HEADS), jnp.float32, 1.0, 16.0))
    dt = jnp.exp(jax.random.uniform(ks[10], (DEPTH, 2, DN_HEADS), jnp.float32,
                                    math.log(1e-3), math.log(1e-1)))
    dt_bias = dt + jnp.log(-jnp.expm1(-dt))
    onorm_g = 1.0 + nrm(ks[11], (DEPTH, DV), 0.02)
    w_out = nrm(ks[12], (DEPTH, MIX_W, D_MODEL), MIX_W ** -0.5)
    norm2_g = 1.0 + nrm(ks[13], (DEPTH, D_MODEL), 0.02)
    w_group = nrm(ks[14], (DEPTH, D_MODEL, N_GROUPS), D_MODEL ** -0.5)
    b_group = nrm(ks[15], (DEPTH, N_GROUPS), 0.01)
    w_router = nrm(ks[16], (DEPTH, D_MODEL, N_EXPERTS), D_MODEL ** -0.5)
    b_router = nrm(ks[17], (DEPTH, N_EXPERTS), 0.01)
    w_gate = nrm(ks[18], (DEPTH, N_EXPERTS, D_MODEL, D_EXPERT), D_MODEL ** -0.5)
    w_up = nrm(ks[19], (DEPTH, N_EXPERTS, D_MODEL, D_EXPERT), D_MODEL ** -0.5)
    w_down = nrm(ks[20], (DEPTH, N_EXPERTS, D_EXPERT, D_MODEL), D_EXPERT ** -0.5)
    final_g = 1.0 + nrm(ks[21], (D_MODEL,), 0.02)
    return {"x": x, "c": c, "ctx": ctx, "c_ctx": c_ctx, "w_mod": w_mod, "b_mod": b_mod,
            "norm1_g": norm1_g, "w_in": w_in, "conv_w": conv_w, "a_log": a_log, "dt_bias": dt_bias,
            "onorm_g": onorm_g, "w_out": w_out, "norm2_g": norm2_g, "w_group": w_group,
            "b_group": b_group, "w_router": w_router, "b_router": b_router, "w_gate": w_gate,
            "w_up": w_up, "w_down": w_down, "final_g": final_g}


def reference(x, c, ctx, c_ctx, w_mod, b_mod, norm1_g, w_in, conv_w, a_log, dt_bias, onorm_g, w_out,
              norm2_g, w_group, b_group, w_router, b_router, w_gate, w_up, w_down, final_g):
    for l in range(DEPTH):
        x, ctx = trunk_layer(x, ctx, c, c_ctx, w_mod[l], b_mod[l], norm1_g[l], w_in[l], conv_w[l], a_log[l],
                             dt_bias[l], onorm_g[l], w_out[l], norm2_g[l], w_group[l], b_group[l],
                             w_router[l], b_router[l], w_gate[l], w_up[l], w_down[l],
                             update_ctx=(l < DEPTH - 1))
    return rmsnorm(x, final_g)
```

```python
import functools
import math

import numpy as np
import jax
import jax.numpy as jnp
from jax import lax
from jax.experimental import pallas as pl
from jax.experimental.pallas import tpu as pltpu

F32 = jnp.float32
BF16 = jnp.bfloat16

GRID_W = 64
F_GROUPS = 4
DN_HEADS = 4
HEAD_W = 128
CHUNK = 64
N_GROUPS = 4
EXPERTS_PER_GROUP = 8
N_EXPERTS = N_GROUPS * EXPERTS_PER_GROUP
EPS = 1e-6

LANES = 128
SUBLANES = 8
VMEM_LIMIT = 56 * 1024 * 1024

TM_IN = 512
TM_OUT = 256
TM_RANK = 512
TM_ROW = 256
MOE_TILE = 256
FFT_TB = 8
CONV_ROWS = 256
CONV_PAD = 72


def _cparams(*sem):
    return pltpu.CompilerParams(dimension_semantics=sem, vmem_limit_bytes=VMEM_LIMIT)


def _silu(v):
    return v * jax.nn.sigmoid(v)


def _dot(a, b):
    return jnp.dot(a, b, preferred_element_type=F32)


def _adaln_kernel(c_ref, w_ref, b_ref, o_ref):
    a = _silu(c_ref[...])
    o_ref[...] = jnp.dot(a, w_ref[...], preferred_element_type=F32,
                         precision=lax.Precision.HIGHEST) + b_ref[...]


def _adaln(cc, w_mod, b_mod):
    rows, d = cc.shape
    n = w_mod.shape[1]
    tn = 1024
    return pl.pallas_call(
        _adaln_kernel,
        grid=(n // tn,),
        in_specs=[pl.BlockSpec((rows, d), lambda j: (0, 0)),
                  pl.BlockSpec((d, tn), lambda j: (0, j)),
                  pl.BlockSpec((1, tn), lambda j: (0, j))],
        out_specs=pl.BlockSpec((rows, tn), lambda j: (0, j)),
        out_shape=jax.ShapeDtypeStruct((rows, n), F32),
        compiler_params=_cparams("arbitrary"),
        name="adaln",
    )(cc, w_mod, b_mod.reshape(1, n))


def _inproj_kernel(x_ref, shift_ref, scale_ref, g_ref, wf_ref, wqkv_ref, wz_ref, wg_ref,
                   alog_ref, dtb_ref, f_ref, qkv_ref, z_ref, gb_ref):
    x = x_ref[...]
    ms = jnp.mean(x * x, axis=-1, keepdims=True)
    h = x * lax.rsqrt(ms + EPS) * g_ref[...]
    h = h * (1.0 + scale_ref[0]) + shift_ref[0]
    hb = h.astype(BF16)
    f_ref[...] = _dot(hb, wf_ref[...])
    qkv_ref[...] = _dot(hb, wqkv_ref[...])
    z_ref[...] = _dot(hb, wz_ref[...])
    gates = _dot(hb, wg_ref[...])
    tm = gates.shape[0]
    lane = lax.broadcasted_iota(jnp.int32, gates.shape, 1)
    pos = lax.broadcasted_iota(jnp.int32, gates.shape, 0) & (CHUNK - 1)
    beta = jax.nn.sigmoid(gates)
    sp_in = gates + dtb_ref[...]
    softplus = jnp.maximum(sp_in, 0.0) + jnp.log1p(jnp.exp(-jnp.abs(sp_in)))
    g = -jnp.exp(alog_ref[...]) * softplus
    g = jnp.where((lane >= 8) & (lane < 16), g, 0.0)
    pre = g
    suf = g
    s = 1
    while s < CHUNK:
        pre = pre + jnp.where(pos >= s, pltpu.roll(pre, s, 0), 0.0)
        suf = suf + jnp.where(pos < CHUNK - s, pltpu.roll(suf, tm - s, 0), 0.0)
        s *= 2
    gb_ref[...] = jnp.where(lane < 8, beta, jnp.where(lane < 12, pre, suf))


def _inproj(x2, shift, scale, norm_g, w_in, a_log, dt_bias, tokens_per_batch):
    n, d = x2.shape
    tm = min(TM_IN, tokens_per_batch)
    f_w = F_GROUPS * HEAD_W
    qkv_w = 3 * DN_HEADS * HEAD_W
    z_w = DN_HEADS * HEAD_W
    wb = w_in.astype(BF16)
    wf = wb[:, :f_w]
    wqkv = wb[:, f_w:f_w + qkv_w]
    wz = wb[:, f_w + qkv_w:f_w + qkv_w + z_w]
    n_gate = 4 * DN_HEADS
    wg = jnp.pad(wb[:, f_w + qkv_w + z_w:], ((0, 0), (0, LANES - n_gate)))
    alog = jnp.pad(a_log.reshape(1, -1), ((0, 0), (8, LANES - 16)))
    dtb = jnp.pad(dt_bias.reshape(1, -1), ((0, 0), (8, LANES - 16)))
    steps_per_batch = tokens_per_batch // tm
    bmap = lambda i: (i // steps_per_batch, 0, 0)
    const = lambda i: (0, 0)
    row = lambda i: (i, 0)
    return pl.pallas_call(
        _inproj_kernel,
        grid=(n // tm,),
        in_specs=[pl.BlockSpec((tm, d), row),
                  pl.BlockSpec((1, 1, d), bmap),
                  pl.BlockSpec((1, 1, d), bmap),
                  pl.BlockSpec((1, d), const),
                  pl.BlockSpec((d, f_w), const),
                  pl.BlockSpec((d, qkv_w), const),
                  pl.BlockSpec((d, z_w), const),
                  pl.BlockSpec((d, LANES), const),
                  pl.BlockSpec((1, LANES), const),
                  pl.BlockSpec((1, LANES), const)],
        out_specs=[pl.BlockSpec((tm, f_w), row),
                   pl.BlockSpec((tm, qkv_w), row),
                   pl.BlockSpec((tm, z_w), row),
                   pl.BlockSpec((tm, LANES), row)],
        out_shape=[jax.ShapeDtypeStruct((n, f_w), F32),
                   jax.ShapeDtypeStruct((n, qkv_w), F32),
                   jax.ShapeDtypeStruct((n, z_w), F32),
                   jax.ShapeDtypeStruct((n, LANES), F32)],
        compiler_params=_cparams("arbitrary"),
        name="inproj",
    )(x2, shift, scale, norm_g.reshape(1, d), wf, wqkv, wz, wg, alog, dtb)


def _conv_kernel(x_ref, w_ref, o_ref, pad_ref, *, grid_mode):
    t = x_ref.shape[1]
    j = pl.program_id(1)
    zeros = jnp.zeros((CONV_PAD, LANES), F32)
    pad_ref[0:CONV_PAD, :] = zeros
    pad_ref[CONV_PAD + t:CONV_PAD + t + CONV_PAD, :] = zeros
    pad_ref[CONV_PAD:CONV_PAD + t, :] = x_ref[0]
    rows = min(CONV_ROWS, t)
    col = lax.broadcasted_iota(jnp.int32, (rows, LANES), 0) & (GRID_W - 1)
    if grid_mode:
        taps = [(dr, dc) for dr in (-1, 0, 1) for dc in (-1, 0, 1)]
    else:
        taps = [(0, dc) for dc in (-1, 0, 1)]
    n_heads_qk = 2 * DN_HEADS
    for r0 in range(0, t, rows):
        acc = jnp.zeros((rows, LANES), F32)
        for dr, dc in taps:
            off = CONV_PAD + r0 + GRID_W * dr + dc
            wrow = w_ref[3 * (dr + 1) + (dc + 1):3 * (dr + 1) + (dc + 1) + 1, :]
            term = pad_ref[off:off + rows, :] * wrow
            if grid_mode and dc == -1:
                term = jnp.where(col != 0, term, 0.0)
            if grid_mode and dc == 1:
                term = jnp.where(col != GRID_W - 1, term, 0.0)
            acc = acc + term
        y = _silu(acc)
        inv = lax.rsqrt(jnp.sum(y * y, axis=-1, keepdims=True) + EPS)
        fac = jnp.where(j < DN_HEADS, inv * (HEAD_W ** -0.5), jnp.where(j < n_heads_qk, inv, 1.0))
        o_ref[0, r0:r0 + rows, :] = y * fac


def _conv_features(qkv, conv_w, grid_mode):
    b, t, ch = qkv.shape
    w9 = conv_w.reshape(9, ch)
    return pl.pallas_call(
        functools.partial(_conv_kernel, grid_mode=grid_mode),
        grid=(b, ch // LANES),
        in_specs=[pl.BlockSpec((1, t, LANES), lambda i, j: (i, 0, j)),
                  pl.BlockSpec((9, LANES), lambda i, j: (0, j))],
        out_specs=pl.BlockSpec((1, t, LANES), lambda i, j: (i, 0, j)),
        out_shape=jax.ShapeDtypeStruct((b, t, ch), F32),
        scratch_shapes=[pltpu.VMEM((t + 2 * CONV_PAD, LANES), F32)],
        compiler_params=_cparams("arbitrary", "arbitrary"),
        name="conv_grid" if grid_mode else "conv_seq",
    )(qkv, w9)


def _delta_kernel(q_ref, k_ref, v_ref, gb_ref, grow_ref, s0_ref, o_ref, sfin_ref, s_scr, *,
                  reverse, cb):
    j = pl.program_id(1)
    nh = DN_HEADS
    rr = nh * CHUNK

    @pl.when(j == 0)
    def _():
        s_scr[...] = s0_ref[0]

    lane_b = nh if reverse else 0
    lane_g = 8 + (nh if reverse else 0)
    ri = lax.broadcasted_iota(jnp.int32, (rr, rr), 0)
    ci = lax.broadcasted_iota(jnp.int32, (rr, rr), 1)
    same = (ri >> 6) == (ci >> 6)
    if reverse:
        tri = same & (ri <= ci)
        strict = same & (ri < ci)
    else:
        tri = same & (ri >= ci)
        strict = same & (ri > ci)
    eye = (ri == ci).astype(F32)

    order = range(cb - 1, -1, -1) if reverse else range(cb)
    for c in order:
        rows = slice(c * CHUNK, (c + 1) * CHUNK)
        q = q_ref[0, rows, :]
        k = k_ref[0, rows, :]
        v = v_ref[0, rows, :]
        gb = gb_ref[0, rows, :]
        grow = grow_ref[0, c:c + 1, :]
        kb_l, qc_l, kc_l, vb_l, kbe_l, qg_l, kd_l, gcb_l, egl_l = [], [], [], [], [], [], [], [], []
        for h in range(nh):
            hs = slice(h * HEAD_W, (h + 1) * HEAD_W)
            beta = jnp.broadcast_to(gb[:, lane_b + h:lane_b + h + 1], (CHUNK, HEAD_W))
            gc = jnp.broadcast_to(gb[:, lane_g + h:lane_g + h + 1], (CHUNK, HEAD_W))
            glast = gc[0:1, :] if reverse else gc[CHUNK - 1:CHUNK, :]
            eg = jnp.exp(gc)
            kh = k[:, hs]
            kbh = kh * beta
            kb_l.append(kbh)
            kc_l.append(kh)
            qc_l.append(q[:, hs])
            vb_l.append(v[:, hs] * beta)
            kbe_l.append(kbh * eg)
            qg_l.append(q[:, hs] * eg)
            kd_l.append(kh * jnp.exp(glast - gc))
            gcb_l.append(gc)
            egl_l.append(jnp.exp(glast))
        kc = jnp.concatenate(kc_l, axis=0).astype(BF16)
        lhs = jnp.concatenate(kb_l + qc_l, axis=0).astype(BF16)
        a = lax.dot_general(lhs, kc, (((1,), (1,)), ((), ())), preferred_element_type=F32)
        gcb = jnp.concatenate(gcb_l, axis=0)
        gcol = jnp.concatenate([gcb] * (rr // HEAD_W), axis=1)
        diff = gcol - grow
        dec = jnp.where(tri, jnp.exp(jnp.where(tri, diff, 0.0)), 0.0)
        nm = -jnp.where(strict, a[:rr] * dec, 0.0)
        intra = a[rr:] * dec
        p = eye + nm
        xp = nm.astype(BF16)
        step = 1
        while step < CHUNK // 2:
            xsq = _dot(xp, xp)
            xp = xsq.astype(BF16)
            p = p + _dot(p.astype(BF16), xp)
            step *= 2
        rhs = jnp.concatenate([jnp.concatenate(vb_l, axis=0), jnp.concatenate(kbe_l, axis=0)],
                              axis=1).astype(BF16)
        uw = _dot(p.astype(BF16), rhs)
        uwb = uw.astype(BF16)
        iw = _dot(intra.astype(BF16), uwb)
        qg = jnp.concatenate(qg_l, axis=0)
        qe = qg - iw[:, HEAD_W:]
        for h in range(nh):
            hr = slice(h * CHUNK, (h + 1) * CHUNK)
            kdt = kd_l[h].T.astype(BF16)
            gq = _dot(kdt, uwb[hr, :])
            s_h = s_scr[h]
            rs = _dot(jnp.concatenate([gq[:, HEAD_W:], qe[hr, :]], axis=0).astype(BF16),
                      s_h.astype(BF16))
            o_ref[0, rows, h * HEAD_W:(h + 1) * HEAD_W] = iw[hr, :HEAD_W] + rs[HEAD_W:]
            s_scr[h] = egl_l[h] * s_h - rs[:HEAD_W] + gq[:, :HEAD_W]

    @pl.when(j == pl.num_programs(1) - 1)
    def _():
        sfin_ref[0] = s_scr[...]


def _delta_scan(qkv, gb, s0, reverse):
    b, t, _ = qkv.shape
    n_chunks = t // CHUNK
    cb = 8 if n_chunks % 8 == 0 else 4
    nb = n_chunks // cb
    lane0 = 8 + (DN_HEADS if reverse else 0)
    grow = gb[:, :, lane0:lane0 + DN_HEADS].reshape(b, n_chunks, CHUNK, DN_HEADS)
    grow = grow.transpose(0, 1, 3, 2).reshape(b * nb, cb, DN_HEADS * CHUNK)
    w = DN_HEADS * HEAD_W
    blk = (lambda j: nb - 1 - j) if reverse else (lambda j: j)
    return pl.pallas_call(
        functools.partial(_delta_kernel, reverse=reverse, cb=cb),
        grid=(b, nb),
        in_specs=[pl.BlockSpec((1, cb * CHUNK, w), lambda i, j: (i, blk(j), 0)),
                  pl.BlockSpec((1, cb * CHUNK, w), lambda i, j: (i, blk(j), 1)),
                  pl.BlockSpec((1, cb * CHUNK, w), lambda i, j: (i, blk(j), 2)),
                  pl.BlockSpec((1, cb * CHUNK, LANES), lambda i, j: (i, blk(j), 0)),
                  pl.BlockSpec((1, cb, DN_HEADS * CHUNK), lambda i, j: (i * nb + blk(j), 0, 0)),
                  pl.BlockSpec((1, DN_HEADS, HEAD_W, HEAD_W), lambda i, j: (i, 0, 0, 0))],
        out_specs=[pl.BlockSpec((1, cb * CHUNK, w), lambda i, j: (i, blk(j), 0)),
                   pl.BlockSpec((1, DN_HEADS, HEAD_W, HEAD_W), lambda i, j: (i, 0, 0, 0))],
        out_shape=[jax.ShapeDtypeStruct((b, t, w), F32),
                   jax.ShapeDtypeStruct((b, DN_HEADS, HEAD_W, HEAD_W), F32)],
        scratch_shapes=[pltpu.VMEM((DN_HEADS, HEAD_W, HEAD_W), F32)],
        compiler_params=_cparams("arbitrary", "arbitrary"),
        name="delta_bwd" if reverse else "delta_fwd",
    )(qkv, qkv, qkv, gb, grow, s0)


def _dft_tables(t):
    n1 = t // GRID_W
    t1 = np.arange(n1)
    t2 = np.arange(GRID_W)
    ang = 2.0 * np.pi * (np.outer(t1, t1)[None] / n1 + (t2[:, None, None] * t1[None, :, None]) / t)
    ftw = np.concatenate([np.cos(ang), -np.sin(ang)], axis=1)
    a2 = 2.0 * np.pi * np.outer(t2, t2) / GRID_W
    c2, s2 = np.cos(a2), np.sin(a2)
    f2 = np.block([[c2, s2], [-s2, c2]])
    ch = np.arange(HEAD_W)
    a3 = 2.0 * np.pi * np.outer(ch, ch) / HEAD_W
    f3 = np.concatenate([np.cos(a3), np.sin(a3)], axis=0) / math.sqrt(t * HEAD_W)
    return (jnp.asarray(ftw, BF16), jnp.asarray(f2, BF16), jnp.asarray(f3, BF16))


def _dft_rows_kernel(x_ref, f_ref, o_ref):
    n1 = x_ref.shape[1]
    for c in range(FFT_TB):
        xc = x_ref[0, :, c, :].astype(BF16)
        a = _dot(f_ref[c], xc)
        o_ref[0, 0, :, c, :] = a[:n1]
        o_ref[0, 1, :, c, :] = a[n1:]


def _dft_cols_kernel(a_ref, f2_ref, f3_ref, o_ref):
    for kk in range(FFT_TB):
        r = jnp.concatenate([a_ref[0, 0, kk], a_ref[0, 1, kk]], axis=0).astype(BF16)
        g = _dot(f2_ref[...], r)
        gc = jnp.concatenate([g[:GRID_W], g[GRID_W:]], axis=1).astype(BF16)
        o_ref[0, :, kk, :] = _dot(gc, f3_ref[...])


def _fourier_mix(f):
    b, t, w = f.shape
    n1 = t // GRID_W
    ftw, f2, f3 = _dft_tables(t)
    x4 = f.reshape(b, n1, GRID_W, w)
    inter = pl.pallas_call(
        _dft_rows_kernel,
        grid=(b, w // LANES, GRID_W // FFT_TB),
        in_specs=[pl.BlockSpec((1, n1, FFT_TB, LANES), lambda i, g, c: (i, 0, c, g)),
                  pl.BlockSpec((FFT_TB, 2 * n1, n1), lambda i, g, c: (c, 0, 0))],
        out_specs=pl.BlockSpec((1, 2, n1, FFT_TB, LANES), lambda i, g, c: (i, 0, 0, c, g)),
        out_shape=jax.ShapeDtypeStruct((b, 2, n1, GRID_W, w), F32),
        compiler_params=_cparams("arbitrary", "arbitrary", "arbitrary"),
        name="dft_rows",
    )(x4, ftw)
    out = pl.pallas_call(
        _dft_cols_kernel,
        grid=(b, w // LANES, n1 // FFT_TB),
        in_specs=[pl.BlockSpec((1, 2, FFT_TB, GRID_W, LANES), lambda i, g, c: (i, 0, c, 0, g)),
                  pl.BlockSpec((2 * GRID_W, 2 * GRID_W), lambda i, g, c: (0, 0)),
                  pl.BlockSpec((2 * HEAD_W, HEAD_W), lambda i, g, c: (0, 0))],
        out_specs=pl.BlockSpec((1, GRID_W, FFT_TB, LANES), lambda i, g, c: (i, 0, c, g)),
        out_shape=jax.ShapeDtypeStruct((b, GRID_W, n1, w), F32),
        compiler_params=_cparams("arbitrary", "arbitrary", "arbitrary"),
        name="dft_cols",
    )(inter, f2, f3)
    return out.reshape(b, t, w)


def _outproj_kernel(fo_ref, of_ref, ob_ref, z_ref, x_ref, gate_ref, shift_ref, scale_ref,
                    og_ref, n2_ref, wo_ref, wrh_ref, wrl_ref, br_ref, x1_ref, h2_ref, rt_ref):
    o = of_ref[...] + ob_ref[...]
    z = z_ref[...]
    parts = [fo_ref[...].astype(BF16)]
    for h in range(DN_HEADS):
        hs = slice(h * HEAD_W, (h + 1) * HEAD_W)
        oh = o[:, hs]
        ms = jnp.mean(oh * oh, axis=-1, keepdims=True)
        on = oh * lax.rsqrt(ms + EPS) * og_ref[...]
        parts.append((on * _silu(z[:, hs])).astype(BF16))
    mix = jnp.concatenate(parts, axis=1)
    y = _dot(mix, wo_ref[...])
    x1 = x_ref[...] + gate_ref[0] * y
    x1_ref[...] = x1
    ms = jnp.mean(x1 * x1, axis=-1, keepdims=True)
    h2 = x1 * lax.rsqrt(ms + EPS) * n2_ref[...]
    h2 = h2 * (1.0 + scale_ref[0]) + shift_ref[0]
    h2_ref[...] = h2
    hh = h2.astype(BF16)
    hl = (h2 - hh.astype(F32)).astype(BF16)
    lg = _dot(hh, wrh_ref[...]) + _dot(hl, wrh_ref[...]) + _dot(hh, wrl_ref[...]) + br_ref[...]
    lane = lax.broadcasted_iota(jnp.int32, lg.shape, 1)
    lane_f = lane.astype(F32)
    neg = jnp.float32(-3.0e38)
    big = jnp.float32(1.0e6)
    is_g = lane < N_GROUPS
    mg = jnp.max(jnp.where(is_g, lg, neg), axis=-1, keepdims=True)
    sg = jnp.sum(jnp.where(is_g, jnp.exp(jnp.where(is_g, lg - mg, 0.0)), 0.0), axis=-1, keepdims=True)
    g_top = jnp.min(jnp.where(is_g & (lg == mg), lane_f, big), axis=-1, keepdims=True)
    pg_top = 1.0 / sg
    eidx = lane - N_GROUPS
    eidx_f = eidx.astype(F32)
    grp_f = (eidx >> 3).astype(F32)
    in_grp = (eidx >= 0) & (eidx < N_EXPERTS) & (grp_f == g_top)
    m1 = jnp.max(jnp.where(in_grp, lg, neg), axis=-1, keepdims=True)
    i1 = jnp.min(jnp.where(in_grp & (lg == m1), eidx_f, big), axis=-1, keepdims=True)
    rest = in_grp & (eidx_f != i1)
    m2 = jnp.max(jnp.where(rest, lg, neg), axis=-1, keepdims=True)
    i2 = jnp.min(jnp.where(rest & (lg == m2), eidx_f, big), axis=-1, keepdims=True)
    e2 = jnp.exp(m2 - m1)
    w0 = pg_top / (1.0 + e2)
    w1 = pg_top * e2 / (1.0 + e2)
    rt_ref[...] = jnp.where(lane == 0, i1,
                            jnp.where(lane == 1, i2,
                                      jnp.where(lane == 2, w0, jnp.where(lane == 3, w1, 0.0))))


def _outproj(fo, of, ob, z, x2, gate1, shift2, scale2, onorm_g, norm2_g, w_out, w_group, b_group,
             w_router, b_router, tokens_per_batch):
    n, d = x2.shape
    tm = TM_OUT
    w = fo.shape[1]
    wr = jnp.pad(jnp.concatenate([w_group, w_router], axis=1), ((0, 0), (0, LANES - N_GROUPS - N_EXPERTS)))
    wrh = wr.astype(BF16)
    wrl = (wr - wrh.astype(F32)).astype(BF16)
    br = jnp.pad(jnp.concatenate([b_group, b_router]).reshape(1, -1), ((0, 0), (0, LANES - N_GROUPS - N_EXPERTS)))
    steps_per_batch = tokens_per_batch // tm
    bmap = lambda i: (i // steps_per_batch, 0, 0)
    const = lambda i: (0, 0)
    row = lambda i: (i, 0)
    return pl.pallas_call(
        _outproj_kernel,
        grid=(n // tm,),
        in_specs=[pl.BlockSpec((tm, w), row), pl.BlockSpec((tm, w), row), pl.BlockSpec((tm, w), row),
                  pl.BlockSpec((tm, w), row), pl.BlockSpec((tm, d), row),
                  pl.BlockSpec((1, 1, d), bmap), pl.BlockSpec((1, 1, d), bmap), pl.BlockSpec((1, 1, d), bmap),
                  pl.BlockSpec((1, HEAD_W), const), pl.BlockSpec((1, d), const),
                  pl.BlockSpec((d, d), const), pl.BlockSpec((d, LANES), const), pl.BlockSpec((d, LANES), const),
                  pl.BlockSpec((1, LANES), const)],
        out_specs=[pl.BlockSpec((tm, d), row), pl.BlockSpec((tm, d), row), pl.BlockSpec((tm, LANES), row)],
        out_shape=[jax.ShapeDtypeStruct((n, d), F32), jax.ShapeDtypeStruct((n, d), F32),
                   jax.ShapeDtypeStruct((n, LANES), F32)],
        compiler_params=_cparams("arbitrary"),
        name="outproj_router",
    )(fo, of, ob, z, x2, gate1, shift2, scale2, onorm_g.reshape(1, HEAD_W), norm2_g.reshape(1, d),
      w_out.astype(BF16), wrh, wrl, br)


def _rank_kernel(rt_ref, rk_ref, cnt_ref, carry_ref):
    i = pl.program_id(0)

    @pl.when(i == 0)
    def _():
        carry_ref[...] = jnp.zeros_like(carry_ref)

    rt = rt_ref[...]
    tm = rt.shape[0]
    lane = lax.broadcasted_iota(jnp.int32, rt.shape, 1)
    e0 = rt[:, 0:1].astype(jnp.int32)
    e1 = rt[:, 1:2].astype(jnp.int32)
    oh0 = (lane == e0).astype(F32)
    oh1 = (lane == e1).astype(F32)
    both = oh0 + oh1
    r = lax.broadcasted_iota(jnp.int32, (tm, tm), 0)
    c = lax.broadcasted_iota(jnp.int32, (tm, tm), 1)
    lower = (r > c).astype(BF16)
    before = _dot(lower, both.astype(BF16)) + carry_ref[0:1, :]
    rank0 = jnp.sum(before * oh0, axis=-1, keepdims=True)
    rank1 = jnp.sum(before * oh1, axis=-1, keepdims=True)
    rk_ref[...] = jnp.where(lane == 0, rank0, jnp.where(lane == 1, rank1, 0.0))
    total = carry_ref[0:1, :] + jnp.sum(both, axis=0, keepdims=True)
    carry_ref[...] = jnp.broadcast_to(total, carry_ref.shape)
    cnt_ref[...] = jnp.broadcast_to(total, cnt_ref.shape)


def _rank(rt):
    n = rt.shape[0]
    tm = min(TM_RANK, n)
    return pl.pallas_call(
        _rank_kernel,
        grid=(n // tm,),
        in_specs=[pl.BlockSpec((tm, LANES), lambda i: (i, 0))],
        out_specs=[pl.BlockSpec((tm, LANES), lambda i: (i, 0)),
                   pl.BlockSpec((SUBLANES, LANES), lambda i: (0, 0))],
        out_shape=[jax.ShapeDtypeStruct((n, LANES), F32), jax.ShapeDtypeStruct((SUBLANES, LANES), F32)],
        scratch_shapes=[pltpu.VMEM((SUBLANES, LANES), F32)],
        compiler_params=_cparams("arbitrary"),
        name="moe_rank",
    )(rt)


def _dest_kernel(rt_ref, rk_ref, ps_ref, d_ref):
    rt = rt_ref[...]
    rk = rk_ref[...]
    lane = lax.broadcasted_iota(jnp.int32, rt.shape, 1)
    e0 = rt[:, 0:1].astype(jnp.int32)
    e1 = rt[:, 1:2].astype(jnp.int32)
    ps = ps_ref[0:1, :]
    d0 = jnp.sum(jnp.where(lane == e0, ps, 0.0), axis=-1, keepdims=True) + rk[:, 0:1]
    d1 = jnp.sum(jnp.where(lane == e1, ps, 0.0), axis=-1, keepdims=True) + rk[:, 1:2]
    d_ref[...] = jnp.where(lane == 0, d0, jnp.where(lane == 1, d1, 0.0)).astype(jnp.int32)


def _dest(rt, rk, pstart_row):
    n = rt.shape[0]
    tm = min(TM_RANK, n)
    return pl.pallas_call(
        _dest_kernel,
        grid=(n // tm,),
        in_specs=[pl.BlockSpec((tm, LANES), lambda i: (i, 0)),
                  pl.BlockSpec((tm, LANES), lambda i: (i, 0)),
                  pl.BlockSpec((SUBLANES, LANES), lambda i: (0, 0))],
        out_specs=pl.BlockSpec((tm, LANES), lambda i: (i, 0)),
        out_shape=jax.ShapeDtypeStruct((n, LANES), jnp.int32),
        compiler_params=_cparams("arbitrary"),
        name="moe_dest",
    )(rt, rk, pstart_row)


def _dispatch_kernel(d0_ref, d1_ref, h_ref, xs_in_ref, xs_ref, sem):
    del xs_in_ref
    tm = h_ref.shape[0]

    def issue(t, carry):
        pltpu.make_async_copy(h_ref.at[pl.ds(t, 1)], xs_ref.at[pl.ds(d0_ref[0, 0, t], 1)], sem).start()
        pltpu.make_async_copy(h_ref.at[pl.ds(t, 1)], xs_ref.at[pl.ds(d1_ref[0, 0, t], 1)], sem).start()
        return carry

    lax.fori_loop(0, tm, issue, 0)

    def drain(t, carry):
        pltpu.make_async_copy(h_ref.at[pl.ds(0, 1)], xs_ref.at[pl.ds(0, 1)], sem).wait()
        pltpu.make_async_copy(h_ref.at[pl.ds(0, 1)], xs_ref.at[pl.ds(0, 1)], sem).wait()
        return carry

    lax.fori_loop(0, tm, drain, 0)


def _dispatch(h2, d0, d1, n_slots):
    n, d = h2.shape
    tm = min(TM_ROW, n)
    xs0 = jnp.zeros((n_slots, d), F32)
    smem = lambda: pl.BlockSpec((1, 1, tm), lambda i: (i, 0, 0), memory_space=pltpu.SMEM)
    return pl.pallas_call(
        _dispatch_kernel,
        grid=(n // tm,),
        in_specs=[smem(), smem(),
                  pl.BlockSpec((tm, d), lambda i: (i, 0)),
                  pl.BlockSpec(memory_space=pl.ANY)],
        out_specs=pl.BlockSpec(memory_space=pl.ANY),
        out_shape=jax.ShapeDtypeStruct((n_slots, d), F32),
        scratch_shapes=[pltpu.SemaphoreType.DMA(())],
        input_output_aliases={3: 0},
        compiler_params=_cparams("arbitrary"),
        name="moe_dispatch",
    )(d0.reshape(n // tm, 1, tm), d1.reshape(n // tm, 1, tm), h2, xs0)


def _expert_kernel(be_ref, na_ref, xs_ref, wg_ref, wu_ref, wd_ref, ys_ref, wgb, wub, wdb):
    i = pl.program_id(0)
    prev = be_ref[jnp.maximum(i - 1, 0)]
    fresh = (i == 0) | (be_ref[i] != prev)
    active = i < na_ref[0]

    @pl.when(active & fresh)
    def _():
        wgb[...] = wg_ref[0].astype(BF16)
        wub[...] = wu_ref[0].astype(BF16)
        wdb[...] = wd_ref[0].astype(BF16)

    @pl.when(active)
    def _():
        x = xs_ref[...].astype(BF16)
        hid = _silu(_dot(x, wgb[...])) * _dot(x, wub[...])
        ys_ref[...] = _dot(hid.astype(BF16), wdb[...])

    @pl.when(jnp.logical_not(active))
    def _():
        ys_ref[...] = jnp.zeros_like(ys_ref)


def _experts(xs, blk_expert, n_active, w_gate, w_up, w_down):
    n_slots, d = xs.shape
    n_blocks = n_slots // MOE_TILE
    de = w_gate.shape[2]
    xmap = lambda i, be, na: (jnp.minimum(i, na[0] - 1), 0)
    wmap = lambda i, be, na: (be[i], 0, 0)
    grid_spec = pltpu.PrefetchScalarGridSpec(
        num_scalar_prefetch=2,
        grid=(n_blocks,),
        in_specs=[pl.BlockSpec((MOE_TILE, d), xmap),
                  pl.BlockSpec((1, d, de), wmap),
                  pl.BlockSpec((1, d, de), wmap),
                  pl.BlockSpec((1, de, d), wmap)],
        out_specs=pl.BlockSpec((MOE_TILE, d), lambda i, be, na: (i, 0)),
        scratch_shapes=[pltpu.VMEM((d, de), BF16), pltpu.VMEM((d, de), BF16), pltpu.VMEM((de, d), BF16)],
    )
    return pl.pallas_call(
        _expert_kernel,
        grid_spec=grid_spec,
        out_shape=jax.ShapeDtypeStruct((n_slots, d), F32),
        compiler_params=_cparams("arbitrary"),
        name="moe_experts",
    )(blk_expert, n_active, xs, w_gate, w_up, w_down)


def _combine_kernel(d0_ref, d1_ref, ys_ref, rt_ref, x1_ref, gate_ref, fg_ref, o_ref, ya, yb, sem):
    tm = x1_ref.shape[0]

    def issue(t, carry):
        pltpu.make_async_copy(ys_ref.at[pl.ds(d0_ref[0, 0, t], 1)], ya.at[pl.ds(t, 1)], sem).start()
        pltpu.make_async_copy(ys_ref.at[pl.ds(d1_ref[0, 0, t], 1)], yb.at[pl.ds(t, 1)], sem).start()
        return carry

    lax.fori_loop(0, tm, issue, 0)

    def drain(t, carry):
        pltpu.make_async_copy(ys_ref.at[pl.ds(0, 1)], ya.at[pl.ds(0, 1)], sem).wait()
        pltpu.make_async_copy(ys_ref.at[pl.ds(0, 1)], yb.at[pl.ds(0, 1)], sem).wait()
        return carry

    lax.fori_loop(0, tm, drain, 0)
    rt = rt_ref[...]
    moe = rt[:, 2:3] * ya[...] + rt[:, 3:4] * yb[...]
    xo = x1_ref[...] + gate_ref[0] * moe
    ms = jnp.mean(xo * xo, axis=-1, keepdims=True)
    o_ref[...] = xo * lax.rsqrt(ms + EPS) * fg_ref[...]


def _combine(ys, d0, d1, rt, x1, gate2, final_g, tokens_per_batch):
    n, d = x1.shape
    tm = min(TM_ROW, n)
    steps_per_batch = tokens_per_batch // tm
    smem = lambda: pl.BlockSpec((1, 1, tm), lambda i: (i, 0, 0), memory_space=pltpu.SMEM)
    return pl.pallas_call(
        _combine_kernel,
        grid=(n // tm,),
        in_specs=[smem(), smem(),
                  pl.BlockSpec(memory_space=pl.ANY),
                  pl.BlockSpec((tm, LANES), lambda i: (i, 0)),
                  pl.BlockSpec((tm, d), lambda i: (i, 0)),
                  pl.BlockSpec((1, 1, d), lambda i: (i // steps_per_batch, 0, 0)),
                  pl.BlockSpec((1, d), lambda i: (0, 0))],
        out_specs=pl.BlockSpec((tm, d), lambda i: (i, 0)),
        out_shape=jax.ShapeDtypeStruct((n, d), F32),
        scratch_shapes=[pltpu.VMEM((tm, d), F32), pltpu.VMEM((tm, d), F32), pltpu.SemaphoreType.DMA(())],
        compiler_params=_cparams("arbitrary"),
        name="moe_combine",
    )(d0.reshape(n // tm, 1, tm), d1.reshape(n // tm, 1, tm), ys, rt, x1, gate2, final_g.reshape(1, d))


def _moe_and_final(h2, rt, x1, gate2, final_g, w_gate, w_up, w_down, tokens_per_batch):
    n, d = h2.shape
    rk, cnt = _rank(rt)
    counts = cnt[0, :N_EXPERTS].astype(jnp.int32)
    pcounts = (counts + MOE_TILE - 1) // MOE_TILE * MOE_TILE
    pend = jnp.cumsum(pcounts)
    pstart = pend - pcounts
    n_slots = -(-(2 * n) // MOE_TILE) * MOE_TILE + N_EXPERTS * MOE_TILE
    n_blocks = n_slots // MOE_TILE
    blk_expert = jnp.minimum(
        jnp.searchsorted(pend, jnp.arange(n_blocks, dtype=jnp.int32) * MOE_TILE, side='right'),
        N_EXPERTS - 1).astype(jnp.int32)
    n_active = (pend[-1:] // MOE_TILE).astype(jnp.int32)
    ps_row = jnp.broadcast_to(jnp.pad(pstart.astype(F32), (0, LANES - N_EXPERTS))[None, :], (SUBLANES, LANES))
    dest = _dest(rt, rk, ps_row)
    d0 = dest[:, 0]
    d1 = dest[:, 1]
    xs = _dispatch(h2, d0, d1, n_slots)
    ys = _experts(xs, blk_expert, n_active, w_gate, w_up, w_down)
    return _combine(ys, d0, d1, rt, x1, gate2, final_g, tokens_per_batch)


def _layer(x, c, ctx, c_ctx, w_mod, b_mod, norm1_g, w_in, conv_w, a_log, dt_bias, onorm_g, w_out,
           norm2_g, w_group, b_group, w_router, b_router, w_gate, w_up, w_down, final_g):
    b, t, d = x.shape
    tc = ctx.shape[1]
    rows = -(-(b + 1) // SUBLANES) * SUBLANES
    cc = jnp.zeros((rows, d), F32).at[:b].set(c).at[b].set(c_ctx)
    mod = _adaln(cc, w_mod, b_mod)
    mx = [mod[:b, i * d:(i + 1) * d].reshape(b, 1, d) for i in range(6)]
    mc = [jnp.broadcast_to(mod[b:b + 1, i * d:(i + 1) * d].reshape(1, 1, d), (b, 1, d)) for i in range(2)]

    _, qkv_c, _, gb_c = _inproj(ctx.reshape(b * tc, d), mc[0], mc[1], norm1_g, w_in, a_log, dt_bias, tc)
    feat_c = _conv_features(qkv_c.reshape(b, tc, -1), conv_w, grid_mode=False)
    gb_c = gb_c.reshape(b, tc, LANES)
    zero_state = jnp.zeros((b, DN_HEADS, HEAD_W, HEAD_W), F32)
    _, s_fwd = _delta_scan(feat_c, gb_c, zero_state, reverse=False)
    _, s_bwd = _delta_scan(feat_c, gb_c, zero_state, reverse=True)

    x2 = x.reshape(b * t, d)
    f, qkv, z, gb = _inproj(x2, mx[0], mx[1], norm1_g, w_in, a_log, dt_bias, t)
    feat = _conv_features(qkv.reshape(b, t, -1), conv_w, grid_mode=True)
    gb = gb.reshape(b, t, LANES)
    o_f, _ = _delta_scan(feat, gb, s_fwd, reverse=False)
    o_b, _ = _delta_scan(feat, gb, s_bwd, reverse=True)
    fo = _fourier_mix(f.reshape(b, t, -1))
    x1, h2, rt = _outproj(fo.reshape(b * t, -1), o_f.reshape(b * t, -1), o_b.reshape(b * t, -1), z, x2,
                          mx[2], mx[3], mx[4], onorm_g, norm2_g, w_out, w_group, b_group, w_router,
                          b_router, t)
    out = _moe_and_final(h2, rt, x1, mx[5], final_g, w_gate, w_up, w_down, t)
    return out.reshape(b, t, d)


def kernel(x, c, ctx, c_ctx, w_mod, b_mod, norm1_g, w_in, conv_w, a_log, dt_bias, onorm_g, w_out, norm2_g,
           w_group, b_group, w_router, b_router, w_gate, w_up, w_down, final_g):
    assert w_mod.shape[0] == 1, "single-layer trunk"
    return _layer(x, c, ctx, c_ctx, w_mod[0], b_mod[0], norm1_g[0], w_in[0], conv_w[0], a_log[0], dt_bias[0],
                  onorm_g[0], w_out[0], norm2_g[0], w_group[0], b_group[0], w_router[0], b_router[0],
                  w_gate[0], w_up[0], w_down[0], final_g)
```

```python
import functools
import math

import numpy as np
import jax
import jax.numpy as jnp
from jax import lax
from jax.experimental import pallas as pl
from jax.experimental.pallas import tpu as pltpu

F32 = jnp.float32
BF16 = jnp.bfloat16

GRID_W = 64
F_GROUPS = 4
DN_HEADS = 4
HEAD_W = 128
CHUNK = 64
N_GROUPS = 4
EXPERTS_PER_GROUP = 8
N_EXPERTS = N_GROUPS * EXPERTS_PER_GROUP
EPS = 1e-6

LANES = 128
SUBLANES = 8
VMEM_LIMIT = 56 * 1024 * 1024

TM_IN = 512
TM_OUT = 256
TM_RANK = 512
TM_ROW = 256
MOE_TILE = 256
ROW_UNROLL = 8
FFT_TB = 8
CONV_ROWS = 256
CONV_PAD = 72


def _cparams(*sem):
    return pltpu.CompilerParams(dimension_semantics=sem, vmem_limit_bytes=VMEM_LIMIT)


def _silu(v):
    return v * jax.nn.sigmoid(v)


def _dot(a, b):
    return jnp.dot(a, b, preferred_element_type=F32)


def _adaln_kernel(c_ref, w_ref, b_ref, o_ref):
    a = _silu(c_ref[...])
    o_ref[...] = jnp.dot(a, w_ref[...], preferred_element_type=F32,
                         precision=lax.Precision.HIGHEST) + b_ref[...]


def _adaln(cc, w_mod, b_mod):
    rows, d = cc.shape
    n = w_mod.shape[1]
    tn = 1024
    return pl.pallas_call(
        _adaln_kernel,
        grid=(n // tn,),
        in_specs=[pl.BlockSpec((rows, d), lambda j: (0, 0)),
                  pl.BlockSpec((d, tn), lambda j: (0, j)),
                  pl.BlockSpec((1, tn), lambda j: (0, j))],
        out_specs=pl.BlockSpec((rows, tn), lambda j: (0, j)),
        out_shape=jax.ShapeDtypeStruct((rows, n), F32),
        compiler_params=_cparams("arbitrary"),
        name="adaln",
    )(cc, w_mod, b_mod.reshape(1, n))


def _inproj_kernel(x_ref, shift_ref, scale_ref, g_ref, wf_ref, wqkv_ref, wz_ref, wg_ref,
                   alog_ref, dtb_ref, f_ref, qkv_ref, z_ref, gb_ref):
    x = x_ref[...]
    ms = jnp.mean(x * x, axis=-1, keepdims=True)
    h = x * lax.rsqrt(ms + EPS) * g_ref[...]
    h = h * (1.0 + scale_ref[0]) + shift_ref[0]
    hb = h.astype(BF16)
    f_ref[...] = _dot(hb, wf_ref[...])
    qkv_ref[...] = _dot(hb, wqkv_ref[...])
    z_ref[...] = _dot(hb, wz_ref[...])
    gates = _dot(hb, wg_ref[...])
    tm = gates.shape[0]
    lane = lax.broadcasted_iota(jnp.int32, gates.shape, 1)
    pos = lax.broadcasted_iota(jnp.int32, gates.shape, 0) & (CHUNK - 1)
    beta = jax.nn.sigmoid(gates)
    sp_in = gates + dtb_ref[...]
    softplus = jnp.maximum(sp_in, 0.0) + jnp.log1p(jnp.exp(-jnp.abs(sp_in)))
    g = -jnp.exp(alog_ref[...]) * softplus
    g = jnp.where((lane >= 8) & (lane < 16), g, 0.0)
    pre = g
    suf = g
    s = 1
    while s < CHUNK:
        pre = pre + jnp.where(pos >= s, pltpu.roll(pre, s, 0), 0.0)
        suf = suf + jnp.where(pos < CHUNK - s, pltpu.roll(suf, tm - s, 0), 0.0)
        s *= 2
    gb_ref[...] = jnp.where(lane < 8, beta, jnp.where(lane < 12, pre, suf))


def _inproj(x2, shift, scale, norm_g, w_in, a_log, dt_bias, tokens_per_batch):
    n, d = x2.shape
    tm = min(TM_IN, tokens_per_batch)
    f_w = F_GROUPS * HEAD_W
    qkv_w = 3 * DN_HEADS * HEAD_W
    z_w = DN_HEADS * HEAD_W
    wb = w_in.astype(BF16)
    wf = wb[:, :f_w]
    wqkv = wb[:, f_w:f_w + qkv_w]
    wz = wb[:, f_w + qkv_w:f_w + qkv_w + z_w]
    n_gate = 4 * DN_HEADS
    wg = jnp.pad(wb[:, f_w + qkv_w + z_w:], ((0, 0), (0, LANES - n_gate)))
    alog = jnp.pad(a_log.reshape(1, -1), ((0, 0), (8, LANES - 16)))
    dtb = jnp.pad(dt_bias.reshape(1, -1), ((0, 0), (8, LANES - 16)))
    steps_per_batch = tokens_per_batch // tm
    bmap = lambda i: (i // steps_per_batch, 0, 0)
    const = lambda i: (0, 0)
    row = lambda i: (i, 0)
    return pl.pallas_call(
        _inproj_kernel,
        grid=(n // tm,),
        in_specs=[pl.BlockSpec((tm, d), row),
                  pl.BlockSpec((1, 1, d), bmap),
                  pl.BlockSpec((1, 1, d), bmap),
                  pl.BlockSpec((1, d), const),
                  pl.BlockSpec((d, f_w), const),
                  pl.BlockSpec((d, qkv_w), const),
                  pl.BlockSpec((d, z_w), const),
                  pl.BlockSpec((d, LANES), const),
                  pl.BlockSpec((1, LANES), const),
                  pl.BlockSpec((1, LANES), const)],
        out_specs=[pl.BlockSpec((tm, f_w), row),
                   pl.BlockSpec((tm, qkv_w), row),
                   pl.BlockSpec((tm, z_w), row),
                   pl.BlockSpec((tm, LANES), row)],
        out_shape=[jax.ShapeDtypeStruct((n, f_w), F32),
                   jax.ShapeDtypeStruct((n, qkv_w), F32),
                   jax.ShapeDtypeStruct((n, z_w), F32),
                   jax.ShapeDtypeStruct((n, LANES), F32)],
        compiler_params=_cparams("arbitrary"),
        name="inproj",
    )(x2, shift, scale, norm_g.reshape(1, d), wf, wqkv, wz, wg, alog, dtb)


def _conv_kernel(x_ref, w_ref, o_ref, pad_ref, *, grid_mode):
    t = x_ref.shape[1]
    j = pl.program_id(1)
    zeros = jnp.zeros((CONV_PAD, LANES), F32)
    pad_ref[0:CONV_PAD, :] = zeros
    pad_ref[CONV_PAD + t:CONV_PAD + t + CONV_PAD, :] = zeros
    pad_ref[CONV_PAD:CONV_PAD + t, :] = x_ref[0]
    rows = min(CONV_ROWS, t)
    col = lax.broadcasted_iota(jnp.int32, (rows, LANES), 0) & (GRID_W - 1)
    if grid_mode:
        taps = [(dr, dc) for dr in (-1, 0, 1) for dc in (-1, 0, 1)]
    else:
        taps = [(0, dc) for dc in (-1, 0, 1)]
    n_heads_qk = 2 * DN_HEADS
    for r0 in range(0, t, rows):
        acc = jnp.zeros((rows, LANES), F32)
        for dr, dc in taps:
            off = CONV_PAD + r0 + GRID_W * dr + dc
            wrow = w_ref[3 * (dr + 1) + (dc + 1):3 * (dr + 1) + (dc + 1) + 1, :]
            term = pad_ref[off:off + rows, :] * wrow
            if grid_mode and dc == -1:
                term = jnp.where(col != 0, term, 0.0)
            if grid_mode and dc == 1:
                term = jnp.where(col != GRID_W - 1, term, 0.0)
            acc = acc + term
        y = _silu(acc)
        inv = lax.rsqrt(jnp.sum(y * y, axis=-1, keepdims=True) + EPS)
        fac = jnp.where(j < DN_HEADS, inv * (HEAD_W ** -0.5), jnp.where(j < n_heads_qk, inv, 1.0))
        o_ref[0, r0:r0 + rows, :] = y * fac


def _conv_features(qkv, conv_w, grid_mode):
    b, t, ch = qkv.shape
    w9 = conv_w.reshape(9, ch)
    return pl.pallas_call(
        functools.partial(_conv_kernel, grid_mode=grid_mode),
        grid=(b, ch // LANES),
        in_specs=[pl.BlockSpec((1, t, LANES), lambda i, j: (i, 0, j)),
                  pl.BlockSpec((9, LANES), lambda i, j: (0, j))],
        out_specs=pl.BlockSpec((1, t, LANES), lambda i, j: (i, 0, j)),
        out_shape=jax.ShapeDtypeStruct((b, t, ch), F32),
        scratch_shapes=[pltpu.VMEM((t + 2 * CONV_PAD, LANES), F32)],
        compiler_params=_cparams("arbitrary", "arbitrary"),
        name="conv_grid" if grid_mode else "conv_seq",
    )(qkv, w9)


def _delta_kernel(q_ref, k_ref, v_ref, gb_ref, grow_ref, s0_ref, o_ref, sfin_ref, s_scr, *,
                  reverse, cb):
    j = pl.program_id(1)
    nh = DN_HEADS
    rr = nh * CHUNK

    @pl.when(j == 0)
    def _():
        s_scr[...] = s0_ref[0]

    lane_b = nh if reverse else 0
    lane_g = 8 + (nh if reverse else 0)
    ri = lax.broadcasted_iota(jnp.int32, (rr, rr), 0)
    ci = lax.broadcasted_iota(jnp.int32, (rr, rr), 1)
    same = (ri >> 6) == (ci >> 6)
    if reverse:
        tri = same & (ri <= ci)
        strict = same & (ri < ci)
    else:
        tri = same & (ri >= ci)
        strict = same & (ri > ci)
    eye = (ri == ci).astype(F32)

    order = list(range(cb - 1, -1, -1) if reverse else range(cb))

    prep = []
    for c in order:
        rows = slice(c * CHUNK, (c + 1) * CHUNK)
        q = q_ref[0, rows, :]
        k = k_ref[0, rows, :]
        v = v_ref[0, rows, :]
        gb = gb_ref[0, rows, :]
        grow = grow_ref[0, c:c + 1, :]
        kb_l, qc_l, kc_l, vb_l, kbe_l, qg_l, kd_l, gcb_l, egl_l = [], [], [], [], [], [], [], [], []
        for h in range(nh):
            hs = slice(h * HEAD_W, (h + 1) * HEAD_W)
            beta = jnp.broadcast_to(gb[:, lane_b + h:lane_b + h + 1], (CHUNK, HEAD_W))
            gc = jnp.broadcast_to(gb[:, lane_g + h:lane_g + h + 1], (CHUNK, HEAD_W))
            glast = gc[0:1, :] if reverse else gc[CHUNK - 1:CHUNK, :]
            eg = jnp.exp(gc)
            kh = k[:, hs]
            kbh = kh * beta
            kb_l.append(kbh)
            kc_l.append(kh)
            qc_l.append(q[:, hs])
            vb_l.append(v[:, hs] * beta)
            kbe_l.append(kbh * eg)
            qg_l.append(q[:, hs] * eg)
            kd_l.append(kh * jnp.exp(glast - gc))
            gcb_l.append(gc)
            egl_l.append(jnp.exp(glast))
        kc = jnp.concatenate(kc_l, axis=0).astype(BF16)
        lhs = jnp.concatenate(kb_l + qc_l, axis=0).astype(BF16)
        a = lax.dot_general(lhs, kc, (((1,), (1,)), ((), ())), preferred_element_type=F32)
        gcb = jnp.concatenate(gcb_l, axis=0)
        gcol = jnp.concatenate([gcb] * (rr // HEAD_W), axis=1)
        diff = gcol - grow
        dec = jnp.where(tri, jnp.exp(jnp.where(tri, diff, 0.0)), 0.0)
        nm = -jnp.where(strict, a[:rr] * dec, 0.0)
        intra = (a[rr:] * dec).astype(BF16)
        rhs = jnp.concatenate([jnp.concatenate(vb_l, axis=0), jnp.concatenate(kbe_l, axis=0)],
                              axis=1).astype(BF16)
        kdt_l = [kd.T.astype(BF16) for kd in kd_l]
        prep.append((rows, nm, intra, rhs, jnp.concatenate(qg_l, axis=0), kdt_l, egl_l))

    p_l = [eye + pr[1] for pr in prep]
    x_l = [pr[1].astype(BF16) for pr in prep]
    step = 1
    while step < CHUNK // 2:
        x_l = [_dot(xp, xp).astype(BF16) for xp in x_l]
        p_l = [p + _dot(p.astype(BF16), xp) for p, xp in zip(p_l, x_l)]
        step *= 2

    loc = []
    for (rows, _, intra, rhs, qg, kdt_l, egl_l), p in zip(prep, p_l):
        uwb = _dot(p.astype(BF16), rhs).astype(BF16)
        iw = _dot(intra, uwb)
        qe = (qg - iw[:, HEAD_W:]).astype(BF16)
        gq_l = [_dot(kdt_l[h], uwb[h * CHUNK:(h + 1) * CHUNK, :]) for h in range(nh)]
        loc.append((rows, iw[:, :HEAD_W], qe, gq_l, egl_l))

    for rows, o_loc, qe, gq_l, egl_l in loc:
        for h in range(nh):
            hr = slice(h * CHUNK, (h + 1) * CHUNK)
            s_h = s_scr[h]
            lhs = jnp.concatenate([gq_l[h][:, HEAD_W:].astype(BF16), qe[hr, :]], axis=0)
            rs = _dot(lhs, s_h.astype(BF16))
            o_ref[0, rows, h * HEAD_W:(h + 1) * HEAD_W] = o_loc[hr, :] + rs[HEAD_W:]
            s_scr[h] = egl_l[h] * s_h - rs[:HEAD_W] + gq_l[h][:, :HEAD_W]

    @pl.when(j == pl.num_programs(1) - 1)
    def _():
        sfin_ref[0] = s_scr[...]


def _delta_scan(qkv, gb, s0, reverse):
    b, t, _ = qkv.shape
    n_chunks = t // CHUNK
    cb = 8 if n_chunks % 8 == 0 else 4
    nb = n_chunks // cb
    lane0 = 8 + (DN_HEADS if reverse else 0)
    grow = gb[:, :, lane0:lane0 + DN_HEADS].reshape(b, n_chunks, CHUNK, DN_HEADS)
    grow = grow.transpose(0, 1, 3, 2).reshape(b * nb, cb, DN_HEADS * CHUNK)
    w = DN_HEADS * HEAD_W
    blk = (lambda j: nb - 1 - j) if reverse else (lambda j: j)
    return pl.pallas_call(
        functools.partial(_delta_kernel, reverse=reverse, cb=cb),
        grid=(b, nb),
        in_specs=[pl.BlockSpec((1, cb * CHUNK, w), lambda i, j: (i, blk(j), 0)),
                  pl.BlockSpec((1, cb * CHUNK, w), lambda i, j: (i, blk(j), 1)),
                  pl.BlockSpec((1, cb * CHUNK, w), lambda i, j: (i, blk(j), 2)),
                  pl.BlockSpec((1, cb * CHUNK, LANES), lambda i, j: (i, blk(j), 0)),
                  pl.BlockSpec((1, cb, DN_HEADS * CHUNK), lambda i, j: (i * nb + blk(j), 0, 0)),
                  pl.BlockSpec((1, DN_HEADS, HEAD_W, HEAD_W), lambda i, j: (i, 0, 0, 0))],
        out_specs=[pl.BlockSpec((1, cb * CHUNK, w), lambda i, j: (i, blk(j), 0)),
                   pl.BlockSpec((1, DN_HEADS, HEAD_W, HEAD_W), lambda i, j: (i, 0, 0, 0))],
        out_shape=[jax.ShapeDtypeStruct((b, t, w), F32),
                   jax.ShapeDtypeStruct((b, DN_HEADS, HEAD_W, HEAD_W), F32)],
        scratch_shapes=[pltpu.VMEM((DN_HEADS, HEAD_W, HEAD_W), F32)],
        compiler_params=_cparams("arbitrary", "arbitrary"),
        name="delta_bwd" if reverse else "delta_fwd",
    )(qkv, qkv, qkv, gb, grow, s0)


def _dft_tables(t):
    n1 = t // GRID_W
    t1 = np.arange(n1)
    t2 = np.arange(GRID_W)
    ang = 2.0 * np.pi * (np.outer(t1, t1)[None] / n1 + (t2[:, None, None] * t1[None, :, None]) / t)
    ftw = np.concatenate([np.cos(ang), -np.sin(ang)], axis=1)
    a2 = 2.0 * np.pi * np.outer(t2, t2) / GRID_W
    c2, s2 = np.cos(a2), np.sin(a2)
    f2 = np.block([[c2, s2], [-s2, c2]])
    ch = np.arange(HEAD_W)
    a3 = 2.0 * np.pi * np.outer(ch, ch) / HEAD_W
    f3 = np.concatenate([np.cos(a3), np.sin(a3)], axis=0) / math.sqrt(t * HEAD_W)
    return (jnp.asarray(ftw, BF16), jnp.asarray(f2, BF16), jnp.asarray(f3, BF16))


def _dft_rows_kernel(x_ref, f_ref, o_ref):
    n1 = x_ref.shape[1]
    for c in range(FFT_TB):
        xc = x_ref[0, :, c, :].astype(BF16)
        a = _dot(f_ref[c], xc)
        o_ref[0, 0, :, c, :] = a[:n1]
        o_ref[0, 1, :, c, :] = a[n1:]


def _dft_cols_kernel(a_ref, f2_ref, f3_ref, o_ref):
    for kk in range(FFT_TB):
        r = jnp.concatenate([a_ref[0, 0, kk], a_ref[0, 1, kk]], axis=0).astype(BF16)
        g = _dot(f2_ref[...], r)
        gc = jnp.concatenate([g[:GRID_W], g[GRID_W:]], axis=1).astype(BF16)
        o_ref[0, :, kk, :] = _dot(gc, f3_ref[...])


def _fourier_mix(f):
    b, t, w = f.shape
    n1 = t // GRID_W
    ftw, f2, f3 = _dft_tables(t)
    x4 = f.reshape(b, n1, GRID_W, w)
    inter = pl.pallas_call(
        _dft_rows_kernel,
        grid=(b, w // LANES, GRID_W // FFT_TB),
        in_specs=[pl.BlockSpec((1, n1, FFT_TB, LANES), lambda i, g, c: (i, 0, c, g)),
                  pl.BlockSpec((FFT_TB, 2 * n1, n1), lambda i, g, c: (c, 0, 0))],
        out_specs=pl.BlockSpec((1, 2, n1, FFT_TB, LANES), lambda i, g, c: (i, 0, 0, c, g)),
        out_shape=jax.ShapeDtypeStruct((b, 2, n1, GRID_W, w), F32),
        compiler_params=_cparams("arbitrary", "arbitrary", "arbitrary"),
        name="dft_rows",
    )(x4, ftw)
    out = pl.pallas_call(
        _dft_cols_kernel,
        grid=(b, w // LANES, n1 // FFT_TB),
        in_specs=[pl.BlockSpec((1, 2, FFT_TB, GRID_W, LANES), lambda i, g, c: (i, 0, c, 0, g)),
                  pl.BlockSpec((2 * GRID_W, 2 * GRID_W), lambda i, g, c: (0, 0)),
                  pl.BlockSpec((2 * HEAD_W, HEAD_W), lambda i, g, c: (0, 0))],
        out_specs=pl.BlockSpec((1, GRID_W, FFT_TB, LANES), lambda i, g, c: (i, 0, c, g)),
        out_shape=jax.ShapeDtypeStruct((b, GRID_W, n1, w), F32),
        compiler_params=_cparams("arbitrary", "arbitrary", "arbitrary"),
        name="dft_cols",
    )(inter, f2, f3)
    return out.reshape(b, t, w)


def _outproj_kernel(fo_ref, of_ref, ob_ref, z_ref, x_ref, gate_ref, shift_ref, scale_ref,
                    og_ref, n2_ref, wo_ref, wrh_ref, wrl_ref, br_ref, x1_ref, h2_ref, rt_ref):
    o = of_ref[...] + ob_ref[...]
    z = z_ref[...]
    parts = [fo_ref[...].astype(BF16)]
    for h in range(DN_HEADS):
        hs = slice(h * HEAD_W, (h + 1) * HEAD_W)
        oh = o[:, hs]
        ms = jnp.mean(oh * oh, axis=-1, keepdims=True)
        on = oh * lax.rsqrt(ms + EPS) * og_ref[...]
        parts.append((on * _silu(z[:, hs])).astype(BF16))
    mix = jnp.concatenate(parts, axis=1)
    y = _dot(mix, wo_ref[...])
    x1 = x_ref[...] + gate_ref[0] * y
    x1_ref[...] = x1
    ms = jnp.mean(x1 * x1, axis=-1, keepdims=True)
    h2 = x1 * lax.rsqrt(ms + EPS) * n2_ref[...]
    h2 = h2 * (1.0 + scale_ref[0]) + shift_ref[0]
    h2_ref[...] = h2
    hh = h2.astype(BF16)
    hl = (h2 - hh.astype(F32)).astype(BF16)
    lg = _dot(hh, wrh_ref[...]) + _dot(hl, wrh_ref[...]) + _dot(hh, wrl_ref[...]) + br_ref[...]
    lane = lax.broadcasted_iota(jnp.int32, lg.shape, 1)
    lane_f = lane.astype(F32)
    neg = jnp.float32(-3.0e38)
    big = jnp.float32(1.0e6)
    is_g = lane < N_GROUPS
    mg = jnp.max(jnp.where(is_g, lg, neg), axis=-1, keepdims=True)
    sg = jnp.sum(jnp.where(is_g, jnp.exp(jnp.where(is_g, lg - mg, 0.0)), 0.0), axis=-1, keepdims=True)
    g_top = jnp.min(jnp.where(is_g & (lg == mg), lane_f, big), axis=-1, keepdims=True)
    pg_top = 1.0 / sg
    eidx = lane - N_GROUPS
    eidx_f = eidx.astype(F32)
    grp_f = (eidx >> 3).astype(F32)
    in_grp = (eidx >= 0) & (eidx < N_EXPERTS) & (grp_f == g_top)
    m1 = jnp.max(jnp.where(in_grp, lg, neg), axis=-1, keepdims=True)
    i1 = jnp.min(jnp.where(in_grp & (lg == m1), eidx_f, big), axis=-1, keepdims=True)
    rest = in_grp & (eidx_f != i1)
    m2 = jnp.max(jnp.where(rest, lg, neg), axis=-1, keepdims=True)
    i2 = jnp.min(jnp.where(rest & (lg == m2), eidx_f, big), axis=-1, keepdims=True)
    e2 = jnp.exp(m2 - m1)
    w0 = pg_top / (1.0 + e2)
    w1 = pg_top * e2 / (1.0 + e2)
    rt_ref[...] = jnp.where(lane == 0, i1,
                            jnp.where(lane == 1, i2,
                                      jnp.where(lane == 2, w0, jnp.where(lane == 3, w1, 0.0))))


def _outproj(fo, of, ob, z, x2, gate1, shift2, scale2, onorm_g, norm2_g, w_out, w_group, b_group,
             w_router, b_router, tokens_per_batch):
    n, d = x2.shape
    tm = TM_OUT
    w = fo.shape[1]
    wr = jnp.pad(jnp.concatenate([w_group, w_router], axis=1), ((0, 0), (0, LANES - N_GROUPS - N_EXPERTS)))
    wrh = wr.astype(BF16)
    wrl = (wr - wrh.astype(F32)).astype(BF16)
    br = jnp.pad(jnp.concatenate([b_group, b_router]).reshape(1, -1), ((0, 0), (0, LANES - N_GROUPS - N_EXPERTS)))
    steps_per_batch = tokens_per_batch // tm
    bmap = lambda i: (i // steps_per_batch, 0, 0)
    const = lambda i: (0, 0)
    row = lambda i: (i, 0)
    return pl.pallas_call(
        _outproj_kernel,
        grid=(n // tm,),
        in_specs=[pl.BlockSpec((tm, w), row), pl.BlockSpec((tm, w), row), pl.BlockSpec((tm, w), row),
                  pl.BlockSpec((tm, w), row), pl.BlockSpec((tm, d), row),
                  pl.BlockSpec((1, 1, d), bmap), pl.BlockSpec((1, 1, d), bmap), pl.BlockSpec((1, 1, d), bmap),
                  pl.BlockSpec((1, HEAD_W), const), pl.BlockSpec((1, d), const),
                  pl.BlockSpec((d, d), const), pl.BlockSpec((d, LANES), const), pl.BlockSpec((d, LANES), const),
                  pl.BlockSpec((1, LANES), const)],
        out_specs=[pl.BlockSpec((tm, d), row), pl.BlockSpec((tm, d), row), pl.BlockSpec((tm, LANES), row)],
        out_shape=[jax.ShapeDtypeStruct((n, d), F32), jax.ShapeDtypeStruct((n, d), F32),
                   jax.ShapeDtypeStruct((n, LANES), F32)],
        compiler_params=_cparams("arbitrary"),
        name="outproj_router",
    )(fo, of, ob, z, x2, gate1, shift2, scale2, onorm_g.reshape(1, HEAD_W), norm2_g.reshape(1, d),
      w_out.astype(BF16), wrh, wrl, br)


def _rank_kernel(rt_ref, rk_ref, cnt_ref, carry_ref):
    i = pl.program_id(0)

    @pl.when(i == 0)
    def _():
        carry_ref[...] = jnp.zeros_like(carry_ref)

    rt = rt_ref[...]
    tm = rt.shape[0]
    lane = lax.broadcasted_iota(jnp.int32, rt.shape, 1)
    e0 = rt[:, 0:1].astype(jnp.int32)
    e1 = rt[:, 1:2].astype(jnp.int32)
    oh0 = (lane == e0).astype(F32)
    oh1 = (lane == e1).astype(F32)
    both = oh0 + oh1
    r = lax.broadcasted_iota(jnp.int32, (tm, tm), 0)
    c = lax.broadcasted_iota(jnp.int32, (tm, tm), 1)
    lower = (r > c).astype(BF16)
    before = _dot(lower, both.astype(BF16)) + carry_ref[0:1, :]
    rank0 = jnp.sum(before * oh0, axis=-1, keepdims=True)
    rank1 = jnp.sum(before * oh1, axis=-1, keepdims=True)
    rk_ref[...] = jnp.where(lane == 0, rank0, jnp.where(lane == 1, rank1, 0.0))
    total = carry_ref[0:1, :] + jnp.sum(both, axis=0, keepdims=True)
    carry_ref[...] = jnp.broadcast_to(total, carry_ref.shape)
    cnt_ref[...] = jnp.broadcast_to(total, cnt_ref.shape)


def _rank(rt):
    n = rt.shape[0]
    tm = min(TM_RANK, n)
    return pl.pallas_call(
        _rank_kernel,
        grid=(n // tm,),
        in_specs=[pl.BlockSpec((tm, LANES), lambda i: (i, 0))],
        out_specs=[pl.BlockSpec((tm, LANES), lambda i: (i, 0)),
                   pl.BlockSpec((SUBLANES, LANES), lambda i: (0, 0))],
        out_shape=[jax.ShapeDtypeStruct((n, LANES), F32), jax.ShapeDtypeStruct((SUBLANES, LANES), F32)],
        scratch_shapes=[pltpu.VMEM((SUBLANES, LANES), F32)],
        compiler_params=_cparams("arbitrary"),
        name="moe_rank",
    )(rt)


def _dest_kernel(rt_ref, rk_ref, ps_ref, d_ref):
    rt = rt_ref[...]
    rk = rk_ref[...]
    lane = lax.broadcasted_iota(jnp.int32, rt.shape, 1)
    e0 = rt[:, 0:1].astype(jnp.int32)
    e1 = rt[:, 1:2].astype(jnp.int32)
    ps = ps_ref[0:1, :]
    d0 = jnp.sum(jnp.where(lane == e0, ps, 0.0), axis=-1, keepdims=True) + rk[:, 0:1]
    d1 = jnp.sum(jnp.where(lane == e1, ps, 0.0), axis=-1, keepdims=True) + rk[:, 1:2]
    d_ref[...] = jnp.where(lane == 0, d0, jnp.where(lane == 1, d1, 0.0)).astype(jnp.int32)


def _dest(rt, rk, pstart_row):
    n = rt.shape[0]
    tm = min(TM_RANK, n)
    return pl.pallas_call(
        _dest_kernel,
        grid=(n // tm,),
        in_specs=[pl.BlockSpec((tm, LANES), lambda i: (i, 0)),
                  pl.BlockSpec((tm, LANES), lambda i: (i, 0)),
                  pl.BlockSpec((SUBLANES, LANES), lambda i: (0, 0))],
        out_specs=pl.BlockSpec((tm, LANES), lambda i: (i, 0)),
        out_shape=jax.ShapeDtypeStruct((n, LANES), jnp.int32),
        compiler_params=_cparams("arbitrary"),
        name="moe_dest",
    )(rt, rk, pstart_row)


def _dispatch_kernel(pend_ref, na_ref, d0_ref, d1_ref, h_ref, xs_ref, zbuf, sem):
    tm = h_ref.shape[0]
    n_blocks = xs_ref.shape[0] // MOE_TILE

    @pl.when(pl.program_id(0) == 0)
    def _():
        zbuf[...] = jnp.zeros_like(zbuf)

        def zero_tile(tile):
            return pltpu.make_async_copy(zbuf, xs_ref.at[pl.ds(pl.multiple_of(tile * MOE_TILE, MOE_TILE), MOE_TILE)], sem)

        def last_tile(e):
            prev_end = jnp.where(e == 0, 0, pend_ref[jnp.maximum(e - 1, 0)])
            return pend_ref[e] > prev_end, pend_ref[e] // MOE_TILE - 1

        def start_e(e, carry):
            nonempty, tile = last_tile(e)

            @pl.when(nonempty)
            def _():
                zero_tile(tile).start()
            return carry

        def wait_e(e, carry):
            nonempty, tile = last_tile(e)

            @pl.when(nonempty)
            def _():
                zero_tile(tile).wait()
            return carry

        def start_t(tile, carry):
            zero_tile(tile).start()
            return carry

        def wait_t(tile, carry):
            zero_tile(tile).wait()
            return carry

        lax.fori_loop(0, N_EXPERTS, start_e, 0)
        lax.fori_loop(na_ref[0], n_blocks, start_t, 0)
        lax.fori_loop(0, N_EXPERTS, wait_e, 0)
        lax.fori_loop(na_ref[0], n_blocks, wait_t, 0)

    def issue(g, carry):
        for u in range(ROW_UNROLL):
            t = g * ROW_UNROLL + u
            pltpu.make_async_copy(h_ref.at[pl.ds(t, 1)], xs_ref.at[pl.ds(d0_ref[0, 0, t], 1)], sem).start()
            pltpu.make_async_copy(h_ref.at[pl.ds(t, 1)], xs_ref.at[pl.ds(d1_ref[0, 0, t], 1)], sem).start()
        return carry

    lax.fori_loop(0, tm // ROW_UNROLL, issue, 0)

    def drain(g, carry):
        for _ in range(2 * ROW_UNROLL):
            pltpu.make_async_copy(h_ref.at[pl.ds(0, 1)], xs_ref.at[pl.ds(0, 1)], sem).wait()
        return carry

    lax.fori_loop(0, tm // ROW_UNROLL, drain, 0)


def _dispatch(h2, d0, d1, pend, n_active, n_slots):
    n, d = h2.shape
    tm = min(TM_ROW, n)
    smem = lambda: pl.BlockSpec((1, 1, tm), lambda i, pe, na: (i, 0, 0), memory_space=pltpu.SMEM)
    grid_spec = pltpu.PrefetchScalarGridSpec(
        num_scalar_prefetch=2,
        grid=(n // tm,),
        in_specs=[smem(), smem(), pl.BlockSpec((tm, d), lambda i, pe, na: (i, 0))],
        out_specs=pl.BlockSpec(memory_space=pl.ANY),
        scratch_shapes=[pltpu.VMEM((MOE_TILE, d), F32), pltpu.SemaphoreType.DMA(())],
    )
    return pl.pallas_call(
        _dispatch_kernel,
        grid_spec=grid_spec,
        out_shape=jax.ShapeDtypeStruct((n_slots, d), F32),
        compiler_params=_cparams("arbitrary"),
        name="moe_dispatch",
    )(pend, n_active, d0.reshape(n // tm, 1, tm), d1.reshape(n // tm, 1, tm), h2)


def _expert_kernel(be_ref, na_ref, xs_ref, wg_ref, wu_ref, wd_ref, ys_ref, wgb, wub, wdb):
    i = pl.program_id(0)
    prev = be_ref[jnp.maximum(i - 1, 0)]
    fresh = (i == 0) | (be_ref[i] != prev)
    active = i < na_ref[0]

    @pl.when(active & fresh)
    def _():
        wgb[...] = wg_ref[0].astype(BF16)
        wub[...] = wu_ref[0].astype(BF16)
        wdb[...] = wd_ref[0].astype(BF16)

    @pl.when(active)
    def _():
        x = xs_ref[...].astype(BF16)
        hid = _silu(_dot(x, wgb[...])) * _dot(x, wub[...])
        ys_ref[...] = _dot(hid.astype(BF16), wdb[...])

    @pl.when(jnp.logical_not(active))
    def _():
        ys_ref[...] = jnp.zeros_like(ys_ref)


def _experts(xs, blk_expert, n_active, w_gate, w_up, w_down):
    n_slots, d = xs.shape
    n_blocks = n_slots // MOE_TILE
    de = w_gate.shape[2]
    xmap = lambda i, be, na: (jnp.minimum(i, na[0] - 1), 0)
    wmap = lambda i, be, na: (be[i], 0, 0)
    grid_spec = pltpu.PrefetchScalarGridSpec(
        num_scalar_prefetch=2,
        grid=(n_blocks,),
        in_specs=[pl.BlockSpec((MOE_TILE, d), xmap),
                  pl.BlockSpec((1, d, de), wmap),
                  pl.BlockSpec((1, d, de), wmap),
                  pl.BlockSpec((1, de, d), wmap)],
        out_specs=pl.BlockSpec((MOE_TILE, d), lambda i, be, na: (i, 0)),
        scratch_shapes=[pltpu.VMEM((d, de), BF16), pltpu.VMEM((d, de), BF16), pltpu.VMEM((de, d), BF16)],
    )
    return pl.pallas_call(
        _expert_kernel,
        grid_spec=grid_spec,
        out_shape=jax.ShapeDtypeStruct((n_slots, d), F32),
        compiler_params=_cparams("arbitrary"),
        name="moe_experts",
    )(blk_expert, n_active, xs, w_gate, w_up, w_down)


def _combine_kernel(d0_ref, d1_ref, ys_ref, rt_ref, x1_ref, gate_ref, fg_ref, o_ref, ya, yb, sem):
    tm = x1_ref.shape[0]

    def issue(g, carry):
        for u in range(ROW_UNROLL):
            t = g * ROW_UNROLL + u
            pltpu.make_async_copy(ys_ref.at[pl.ds(d0_ref[0, 0, t], 1)], ya.at[pl.ds(t, 1)], sem).start()
            pltpu.make_async_copy(ys_ref.at[pl.ds(d1_ref[0, 0, t], 1)], yb.at[pl.ds(t, 1)], sem).start()
        return carry

    lax.fori_loop(0, tm // ROW_UNROLL, issue, 0)

    def drain(g, carry):
        for _ in range(ROW_UNROLL):
            pltpu.make_async_copy(ys_ref.at[pl.ds(0, 1)], ya.at[pl.ds(0, 1)], sem).wait()
            pltpu.make_async_copy(ys_ref.at[pl.ds(0, 1)], yb.at[pl.ds(0, 1)], sem).wait()
        return carry

    lax.fori_loop(0, tm // ROW_UNROLL, drain, 0)
    rt = rt_ref[...]
    moe = rt[:, 2:3] * ya[...] + rt[:, 3:4] * yb[...]
    xo = x1_ref[...] + gate_ref[0] * moe
    ms = jnp.mean(xo * xo, axis=-1, keepdims=True)
    o_ref[...] = xo * lax.rsqrt(ms + EPS) * fg_ref[...]


def _combine(ys, d0, d1, rt, x1, gate2, final_g, tokens_per_batch):
    n, d = x1.shape
    tm = min(TM_ROW, n)
    steps_per_batch = tokens_per_batch // tm
    smem = lambda: pl.BlockSpec((1, 1, tm), lambda i: (i, 0, 0), memory_space=pltpu.SMEM)
    return pl.pallas_call(
        _combine_kernel,
        grid=(n // tm,),
        in_specs=[smem(), smem(),
                  pl.BlockSpec(memory_space=pl.ANY),
                  pl.BlockSpec((tm, LANES), lambda i: (i, 0)),
                  pl.BlockSpec((tm, d), lambda i: (i, 0)),
                  pl.BlockSpec((1, 1, d), lambda i: (i // steps_per_batch, 0, 0)),
                  pl.BlockSpec((1, d), lambda i: (0, 0))],
        out_specs=pl.BlockSpec((tm, d), lambda i: (i, 0)),
        out_shape=jax.ShapeDtypeStruct((n, d), F32),
        scratch_shapes=[pltpu.VMEM((tm, d), F32), pltpu.VMEM((tm, d), F32), pltpu.SemaphoreType.DMA(())],
        compiler_params=_cparams("arbitrary"),
        name="moe_combine",
    )(d0.reshape(n // tm, 1, tm), d1.reshape(n // tm, 1, tm), ys, rt, x1, gate2, final_g.reshape(1, d))


def _moe_and_final(h2, rt, x1, gate2, final_g, w_gate, w_up, w_down, tokens_per_batch):
    n, d = h2.shape
    rk, cnt = _rank(rt)
    counts = cnt[0, :N_EXPERTS].astype(jnp.int32)
    pcounts = (counts + MOE_TILE - 1) // MOE_TILE * MOE_TILE
    pend = jnp.cumsum(pcounts)
    pstart = pend - pcounts
    n_slots = -(-(2 * n) // MOE_TILE) * MOE_TILE + N_EXPERTS * MOE_TILE
    n_blocks = n_slots // MOE_TILE
    blk_start = jnp.arange(n_blocks, dtype=jnp.int32) * MOE_TILE
    blk_expert = jnp.minimum(jnp.sum((pend[None, :] <= blk_start[:, None]).astype(jnp.int32), axis=1),
                             N_EXPERTS - 1)
    n_active = (pend[-1:] // MOE_TILE).astype(jnp.int32)
    ps_row = jnp.broadcast_to(jnp.pad(pstart.astype(F32), (0, LANES - N_EXPERTS))[None, :], (SUBLANES, LANES))
    dest = _dest(rt, rk, ps_row)
    d0 = dest[:, 0]
    d1 = dest[:, 1]
    xs = _dispatch(h2, d0, d1, pend.astype(jnp.int32), n_active, n_slots)
    ys = _experts(xs, blk_expert, n_active, w_gate, w_up, w_down)
    return _combine(ys, d0, d1, rt, x1, gate2, final_g, tokens_per_batch)


def _layer(x, c, ctx, c_ctx, w_mod, b_mod, norm1_g, w_in, conv_w, a_log, dt_bias, onorm_g, w_out,
           norm2_g, w_group, b_group, w_router, b_router, w_gate, w_up, w_down, final_g):
    b, t, d = x.shape
    tc = ctx.shape[1]
    rows = -(-(b + 1) // SUBLANES) * SUBLANES
    cc = jnp.zeros((rows, d), F32).at[:b].set(c).at[b].set(c_ctx)
    mod = _adaln(cc, w_mod, b_mod)
    mx = [mod[:b, i * d:(i + 1) * d].reshape(b, 1, d) for i in range(6)]
    mc = [jnp.broadcast_to(mod[b:b + 1, i * d:(i + 1) * d].reshape(1, 1, d), (b, 1, d)) for i in range(2)]

    _, qkv_c, _, gb_c = _inproj(ctx.reshape(b * tc, d), mc[0], mc[1], norm1_g, w_in, a_log, dt_bias, tc)
    feat_c = _conv_features(qkv_c.reshape(b, tc, -1), conv_w, grid_mode=False)
    gb_c = gb_c.reshape(b, tc, LANES)
    zero_state = jnp.zeros((b, DN_HEADS, HEAD_W, HEAD_W), F32)
    _, s_fwd = _delta_scan(feat_c, gb_c, zero_state, reverse=False)
    _, s_bwd = _delta_scan(feat_c, gb_c, zero_state, reverse=True)

    x2 = x.reshape(b * t, d)
    f, qkv, z, gb = _inproj(x2, mx[0], mx[1], norm1_g, w_in, a_log, dt_bias, t)
    feat = _conv_features(qkv.reshape(b, t, -1), conv_w, grid_mode=True)
    gb = gb.reshape(b, t, LANES)
    o_f, _ = _delta_scan(feat, gb, s_fwd, reverse=False)
    o_b, _ = _delta_scan(feat, gb, s_bwd, reverse=True)
    fo = _fourier_mix(f.reshape(b, t, -1))
    x1, h2, rt = _outproj(fo.reshape(b * t, -1), o_f.reshape(b * t, -1), o_b.reshape(b * t, -1), z, x2,
                          mx[2], mx[3], mx[4], onorm_g, norm2_g, w_out, w_group, b_group, w_router,
                          b_router, t)
    out = _moe_and_final(h2, rt, x1, mx[5], final_g, w_gate, w_up, w_down, t)
    return out.reshape(b, t, d)


def kernel(x, c, ctx, c_ctx, w_mod, b_mod, norm1_g, w_in, conv_w, a_log, dt_bias, onorm_g, w_out, norm2_g,
           w_group, b_group, w_router, b_router, w_gate, w_up, w_down, final_g):
    assert w_mod.shape[0] == 1, "single-layer trunk"
    return _layer(x, c, ctx, c_ctx, w_mod[0], b_mod[0], norm1_g[0], w_in[0], conv_w[0], a_log[0], dt_bias[0],
                  onorm_g[0], w_out[0], norm2_g[0], w_group[0], b_group[0], w_router[0], b_router[0],
                  w_gate[0], w_up[0], w_down[0], final_g)
```

```python
import functools
import math

import numpy as np
import jax
import jax.numpy as jnp
from jax import lax
from jax.experimental import pallas as pl
from jax.experimental.pallas import tpu as pltpu

F32 = jnp.float32
BF16 = jnp.bfloat16

GRID_W = 64
F_GROUPS = 4
DN_HEADS = 4
HEAD_W = 128
CHUNK = 64
N_GROUPS = 4
EXPERTS_PER_GROUP = 8
N_EXPERTS = N_GROUPS * EXPERTS_PER_GROUP
EPS = 1e-6

LANES = 128
SUBLANES = 8
VMEM_LIMIT = 56 * 1024 * 1024

TM_IN = 512
TM_OUT = 256
TM_RANK = 512
TM_ROW = 256
MOE_TILE = 512
ROW_UNROLL = 8
FFT_TB = 8
CONV_ROWS = 256
CONV_PAD = 72


def _cparams(*sem):
    return pltpu.CompilerParams(dimension_semantics=sem, vmem_limit_bytes=VMEM_LIMIT)


def _silu(v):
    return v * jax.nn.sigmoid(v)


def _dot(a, b):
    return jnp.dot(a, b, preferred_element_type=F32)


def _adaln_kernel(c_ref, w_ref, b_ref, o_ref):
    a = _silu(c_ref[...])
    o_ref[...] = jnp.dot(a, w_ref[...], preferred_element_type=F32,
                         precision=lax.Precision.HIGHEST) + b_ref[...]


def _adaln(cc, w_mod, b_mod):
    rows, d = cc.shape
    n = w_mod.shape[1]
    tn = 1024
    return pl.pallas_call(
        _adaln_kernel,
        grid=(n // tn,),
        in_specs=[pl.BlockSpec((rows, d), lambda j: (0, 0)),
                  pl.BlockSpec((d, tn), lambda j: (0, j)),
                  pl.BlockSpec((1, tn), lambda j: (0, j))],
        out_specs=pl.BlockSpec((rows, tn), lambda j: (0, j)),
        out_shape=jax.ShapeDtypeStruct((rows, n), F32),
        compiler_params=_cparams("arbitrary"),
        name="adaln",
    )(cc, w_mod, b_mod.reshape(1, n))


def _modulated_norm(x, g, shift, scale):
    ms = jnp.mean(x * x, axis=-1, keepdims=True)
    h = x * lax.rsqrt(ms + EPS) * g
    return (h * (1.0 + scale) + shift).astype(BF16)


def _gate_features(gates, alog_ref, dtb_ref):
    tm = gates.shape[0]
    lane = lax.broadcasted_iota(jnp.int32, gates.shape, 1)
    pos = lax.broadcasted_iota(jnp.int32, gates.shape, 0) & (CHUNK - 1)
    beta = jax.nn.sigmoid(gates)
    sp_in = gates + dtb_ref[...]
    softplus = jnp.maximum(sp_in, 0.0) + jnp.log1p(jnp.exp(-jnp.abs(sp_in)))
    g = -jnp.exp(alog_ref[...]) * softplus
    g = jnp.where((lane >= 8) & (lane < 16), g, 0.0)
    pre = g
    suf = g
    s = 1
    while s < CHUNK:
        pre = pre + jnp.where(pos >= s, pltpu.roll(pre, s, 0), 0.0)
        suf = suf + jnp.where(pos < CHUNK - s, pltpu.roll(suf, tm - s, 0), 0.0)
        s *= 2
    return jnp.where(lane < 8, beta, jnp.where(lane < 12, pre, suf))


def _inproj_kernel(x_ref, shift_ref, scale_ref, g_ref, wf_ref, wqkv_ref, wz_ref, wg_ref,
                   alog_ref, dtb_ref, f_ref, qkv_ref, z_ref, gb_ref):
    hb = _modulated_norm(x_ref[...], g_ref[...], shift_ref[0], scale_ref[0])
    f_ref[...] = _dot(hb, wf_ref[...])
    qkv_ref[...] = _dot(hb, wqkv_ref[...])
    z_ref[...] = _dot(hb, wz_ref[...]).astype(BF16)
    gb_ref[...] = _gate_features(_dot(hb, wg_ref[...]), alog_ref, dtb_ref)


def _split_w_in(w_in, a_log, dt_bias):
    f_w = F_GROUPS * HEAD_W
    qkv_w = 3 * DN_HEADS * HEAD_W
    z_w = DN_HEADS * HEAD_W
    wb = w_in.astype(BF16)
    wg = jnp.pad(wb[:, f_w + qkv_w + z_w:], ((0, 0), (0, LANES - 4 * DN_HEADS)))
    alog = jnp.pad(a_log.reshape(1, -1), ((0, 0), (8, LANES - 16)))
    dtb = jnp.pad(dt_bias.reshape(1, -1), ((0, 0), (8, LANES - 16)))
    return wb[:, :f_w], wb[:, f_w:f_w + qkv_w], wb[:, f_w + qkv_w:f_w + qkv_w + z_w], wg, alog, dtb


def _inproj(x2, shift, scale, norm_g, w_in, a_log, dt_bias, tokens_per_batch):
    n, d = x2.shape
    tm = min(TM_IN, tokens_per_batch)
    wf, wqkv, wz, wg, alog, dtb = _split_w_in(w_in, a_log, dt_bias)
    f_w, qkv_w, z_w = wf.shape[1], wqkv.shape[1], wz.shape[1]
    steps_per_batch = tokens_per_batch // tm
    bmap = lambda i: (i // steps_per_batch, 0, 0)
    const = lambda i: (0, 0)
    row = lambda i: (i, 0)
    return pl.pallas_call(
        _inproj_kernel,
        grid=(n // tm,),
        in_specs=[pl.BlockSpec((tm, d), row),
                  pl.BlockSpec((1, 1, d), bmap),
                  pl.BlockSpec((1, 1, d), bmap),
                  pl.BlockSpec((1, d), const),
                  pl.BlockSpec((d, f_w), const),
                  pl.BlockSpec((d, qkv_w), const),
                  pl.BlockSpec((d, z_w), const),
                  pl.BlockSpec((d, LANES), const),
                  pl.BlockSpec((1, LANES), const),
                  pl.BlockSpec((1, LANES), const)],
        out_specs=[pl.BlockSpec((tm, f_w), row),
                   pl.BlockSpec((tm, qkv_w), row),
                   pl.BlockSpec((tm, z_w), row),
                   pl.BlockSpec((tm, LANES), row)],
        out_shape=[jax.ShapeDtypeStruct((n, f_w), F32),
                   jax.ShapeDtypeStruct((n, qkv_w), F32),
                   jax.ShapeDtypeStruct((n, z_w), BF16),
                   jax.ShapeDtypeStruct((n, LANES), F32)],
        compiler_params=_cparams("arbitrary"),
        name="inproj",
    )(x2, shift, scale, norm_g.reshape(1, d), wf, wqkv, wz, wg, alog, dtb)


def _conv_rows(src, row0, lanes, w, rows, grid_mode):
    col = lax.broadcasted_iota(jnp.int32, (rows, LANES), 0) & (GRID_W - 1)
    acc = None
    for dc in (0, -1, 1):
        part = None
        for dr in ((-1, 0, 1) if grid_mode else (0,)):
            off = row0 + GRID_W * dr + dc
            tap = 3 * (dr + 1) + (dc + 1)
            term = src[off:off + rows, lanes] * w[tap:tap + 1, :]
            part = term if part is None else part + term
        if grid_mode and dc == -1:
            part = jnp.where(col != 0, part, 0.0)
        if grid_mode and dc == 1:
            part = jnp.where(col != GRID_W - 1, part, 0.0)
        acc = part if acc is None else acc + part
    return _silu(acc)


def _conv_kernel(x_ref, w_ref, o_ref, pad_ref, *, grid_mode):
    t = x_ref.shape[1]
    j = pl.program_id(1)
    zeros = jnp.zeros((CONV_PAD, LANES), F32)
    pad_ref[0:CONV_PAD, :] = zeros
    pad_ref[CONV_PAD + t:CONV_PAD + t + CONV_PAD, :] = zeros
    pad_ref[CONV_PAD:CONV_PAD + t, :] = x_ref[0]
    rows = min(CONV_ROWS, t)
    w = w_ref[...]
    for r0 in range(0, t, rows):
        y = _conv_rows(pad_ref, CONV_PAD + r0, slice(None), w, rows, grid_mode)
        inv = lax.rsqrt(jnp.sum(y * y, axis=-1, keepdims=True) + EPS)
        fac = jnp.where(j < DN_HEADS, inv * (HEAD_W ** -0.5), jnp.where(j < 2 * DN_HEADS, inv, 1.0))
        o_ref[0, r0:r0 + rows, :] = y * fac


def _conv_features(qkv, conv_w, grid_mode):
    b, t, ch = qkv.shape
    w9 = conv_w.reshape(9, ch)
    return pl.pallas_call(
        functools.partial(_conv_kernel, grid_mode=grid_mode),
        grid=(b, ch // LANES),
        in_specs=[pl.BlockSpec((1, t, LANES), lambda i, j: (i, 0, j)),
                  pl.BlockSpec((9, LANES), lambda i, j: (0, j))],
        out_specs=pl.BlockSpec((1, t, LANES), lambda i, j: (i, 0, j)),
        out_shape=jax.ShapeDtypeStruct((b, t, ch), F32),
        scratch_shapes=[pltpu.VMEM((t + 2 * CONV_PAD, LANES), F32)],
        compiler_params=_cparams("arbitrary", "arbitrary"),
        name="conv_grid" if grid_mode else "conv_seq",
    )(qkv, w9)


def _delta_kernel(q_ref, k_ref, v_ref, gb_ref, grow_ref, s0_ref, o_ref, sfin_ref, s_scr, *,
                  reverse, cb):
    j = pl.program_id(1)
    nh = DN_HEADS
    rr = nh * CHUNK

    @pl.when(j == 0)
    def _():
        s_scr[...] = s0_ref[0]

    lane_b = nh if reverse else 0
    lane_g = 8 + (nh if reverse else 0)
    ri = lax.broadcasted_iota(jnp.int32, (rr, rr), 0)
    ci = lax.broadcasted_iota(jnp.int32, (rr, rr), 1)
    same = (ri >> 6) == (ci >> 6)
    if reverse:
        tri = same & (ri <= ci)
        strict = same & (ri < ci)
    else:
        tri = same & (ri >= ci)
        strict = same & (ri > ci)
    eye = (ri == ci).astype(F32)

    order = list(range(cb - 1, -1, -1) if reverse else range(cb))

    prep = []
    for c in order:
        rows = slice(c * CHUNK, (c + 1) * CHUNK)
        q = q_ref[0, rows, :]
        k = k_ref[0, rows, :]
        v = v_ref[0, rows, :]
        gb = gb_ref[0, rows, :]
        grow = grow_ref[0, c:c + 1, :]
        kb_l, qc_l, kc_l, vb_l, kbe_l, qg_l, kd_l, gcb_l, egl_l = [], [], [], [], [], [], [], [], []
        for h in range(nh):
            hs = slice(h * HEAD_W, (h + 1) * HEAD_W)
            beta = jnp.broadcast_to(gb[:, lane_b + h:lane_b + h + 1], (CHUNK, HEAD_W))
            gc = jnp.broadcast_to(gb[:, lane_g + h:lane_g + h + 1], (CHUNK, HEAD_W))
            glast = gc[0:1, :] if reverse else gc[CHUNK - 1:CHUNK, :]
            eg = jnp.exp(gc)
            kh = k[:, hs]
            kbh = kh * beta
            kb_l.append(kbh)
            kc_l.append(kh)
            qc_l.append(q[:, hs])
            vb_l.append(v[:, hs] * beta)
            kbe_l.append(kbh * eg)
            qg_l.append(q[:, hs] * eg)
            kd_l.append(kh * jnp.exp(glast - gc))
            gcb_l.append(gc)
            egl_l.append(jnp.exp(glast))
        kc = jnp.concatenate(kc_l, axis=0).astype(BF16)
        lhs = jnp.concatenate(kb_l + qc_l, axis=0).astype(BF16)
        a = lax.dot_general(lhs, kc, (((1,), (1,)), ((), ())), preferred_element_type=F32)
        gcb = jnp.concatenate(gcb_l, axis=0)
        gcol = jnp.concatenate([gcb] * (rr // HEAD_W), axis=1)
        diff = gcol - grow
        dec = jnp.where(tri, jnp.exp(jnp.where(tri, diff, 0.0)), 0.0)
        nm = -jnp.where(strict, a[:rr] * dec, 0.0)
        intra = (a[rr:] * dec).astype(BF16)
        rhs = jnp.concatenate([jnp.concatenate(vb_l, axis=0), jnp.concatenate(kbe_l, axis=0)],
                              axis=1).astype(BF16)
        kdt_l = [kd.T.astype(BF16) for kd in kd_l]
        prep.append((rows, nm, intra, rhs, jnp.concatenate(qg_l, axis=0), kdt_l, egl_l))

    p_l = [eye + pr[1] for pr in prep]
    x_l = [pr[1].astype(BF16) for pr in prep]
    step = 1
    while step < CHUNK // 2:
        x_l = [_dot(xp, xp).astype(BF16) for xp in x_l]
        p_l = [p + _dot(p.astype(BF16), xp) for p, xp in zip(p_l, x_l)]
        step *= 2

    loc = []
    for (rows, _, intra, rhs, qg, kdt_l, egl_l), p in zip(prep, p_l):
        uwb = _dot(p.astype(BF16), rhs).astype(BF16)
        iw = _dot(intra, uwb)
        qe = (qg - iw[:, HEAD_W:]).astype(BF16)
        gq_l = [_dot(kdt_l[h], uwb[h * CHUNK:(h + 1) * CHUNK, :]) for h in range(nh)]
        loc.append((rows, iw[:, :HEAD_W], qe, gq_l, egl_l))

    for rows, o_loc, qe, gq_l, egl_l in loc:
        for h in range(nh):
            hr = slice(h * CHUNK, (h + 1) * CHUNK)
            s_h = s_scr[h]
            lhs = jnp.concatenate([gq_l[h][:, HEAD_W:].astype(BF16), qe[hr, :]], axis=0)
            rs = _dot(lhs, s_h.astype(BF16))
            o_ref[0, rows, h * HEAD_W:(h + 1) * HEAD_W] = (o_loc[hr, :] + rs[HEAD_W:]).astype(BF16)
            s_scr[h] = egl_l[h] * s_h - rs[:HEAD_W] + gq_l[h][:, :HEAD_W]

    @pl.when(j == pl.num_programs(1) - 1)
    def _():
        sfin_ref[0] = s_scr[...]


def _delta_scan(qkv, gb, s0, reverse):
    b, t, _ = qkv.shape
    n_chunks = t // CHUNK
    cb = 8 if n_chunks % 8 == 0 else 4
    nb = n_chunks // cb
    lane0 = 8 + (DN_HEADS if reverse else 0)
    grow = gb[:, :, lane0:lane0 + DN_HEADS].reshape(b, n_chunks, CHUNK, DN_HEADS)
    grow = grow.transpose(0, 1, 3, 2).reshape(b * nb, cb, DN_HEADS * CHUNK)
    w = DN_HEADS * HEAD_W
    blk = (lambda j: nb - 1 - j) if reverse else (lambda j: j)
    return pl.pallas_call(
        functools.partial(_delta_kernel, reverse=reverse, cb=cb),
        grid=(b, nb),
        in_specs=[pl.BlockSpec((1, cb * CHUNK, w), lambda i, j: (i, blk(j), 0)),
                  pl.BlockSpec((1, cb * CHUNK, w), lambda i, j: (i, blk(j), 1)),
                  pl.BlockSpec((1, cb * CHUNK, w), lambda i, j: (i, blk(j), 2)),
                  pl.BlockSpec((1, cb * CHUNK, LANES), lambda i, j: (i, blk(j), 0)),
                  pl.BlockSpec((1, cb, DN_HEADS * CHUNK), lambda i, j: (i * nb + blk(j), 0, 0)),
                  pl.BlockSpec((1, DN_HEADS, HEAD_W, HEAD_W), lambda i, j: (i, 0, 0, 0))],
        out_specs=[pl.BlockSpec((1, cb * CHUNK, w), lambda i, j: (i, blk(j), 0)),
                   pl.BlockSpec((1, DN_HEADS, HEAD_W, HEAD_W), lambda i, j: (i, 0, 0, 0))],
        out_shape=[jax.ShapeDtypeStruct((b, t, w), BF16),
                   jax.ShapeDtypeStruct((b, DN_HEADS, HEAD_W, HEAD_W), F32)],
        scratch_shapes=[pltpu.VMEM((DN_HEADS, HEAD_W, HEAD_W), F32)],
        compiler_params=_cparams("arbitrary", "arbitrary"),
        name="delta_bwd" if reverse else "delta_fwd",
    )(qkv, qkv, qkv, gb, grow, s0)


def _dft_tables(t):
    n1 = t // GRID_W
    t1 = np.arange(n1)
    t2 = np.arange(GRID_W)
    ang = 2.0 * np.pi * (np.outer(t1, t1)[None] / n1 + (t2[:, None, None] * t1[None, :, None]) / t)
    ftw = np.concatenate([np.cos(ang), -np.sin(ang)], axis=1)
    a2 = 2.0 * np.pi * np.outer(t2, t2) / GRID_W
    c2, s2 = np.cos(a2), np.sin(a2)
    f2 = np.block([[c2, s2], [-s2, c2]])
    ch = np.arange(HEAD_W)
    a3 = 2.0 * np.pi * np.outer(ch, ch) / HEAD_W
    f3 = np.concatenate([np.cos(a3), np.sin(a3)], axis=0) / math.sqrt(t * HEAD_W)
    return (jnp.asarray(ftw, BF16), jnp.asarray(f2, BF16), jnp.asarray(f3, BF16))


def _dft_kernel(x_ref, ftw_ref, f2_ref, f3_ref, o_ref, are_ref, aim_ref):
    t = x_ref.shape[0]
    n1 = t // GRID_W
    pitch = are_ref.shape[0] // GRID_W
    for c in range(GRID_W):
        xc = x_ref[pl.ds(c, n1, stride=GRID_W), :].astype(BF16)
        a = _dot(ftw_ref[c], xc)
        are_ref[c * pitch:c * pitch + n1, :] = a[:n1]
        aim_ref[c * pitch:c * pitch + n1, :] = a[n1:]
    tb = min(FFT_TB, n1)
    for k0 in range(0, n1, tb):
        r = jnp.concatenate(
            [jnp.concatenate([are_ref[pl.ds(k0 + kk, GRID_W, stride=pitch), :],
                              aim_ref[pl.ds(k0 + kk, GRID_W, stride=pitch), :]], axis=0).astype(BF16)
             for kk in range(tb)], axis=1)
        g = _dot(f2_ref[...], r)
        gc = jnp.concatenate(
            [jnp.concatenate([g[:GRID_W, kk * LANES:(kk + 1) * LANES], g[GRID_W:, kk * LANES:(kk + 1) * LANES]],
                             axis=1) for kk in range(tb)], axis=0).astype(BF16)
        y = _dot(gc, f3_ref[...])
        for kk in range(tb):
            o_ref[pl.ds(k0 + kk, GRID_W, stride=n1), :] = y[kk * GRID_W:(kk + 1) * GRID_W]


def _fourier_mix(f):
    b, t, w = f.shape
    n1 = t // GRID_W
    ftw, f2, f3 = _dft_tables(t)
    pitch = n1 + SUBLANES
    return pl.pallas_call(
        _dft_kernel,
        grid=(b, w // LANES),
        in_specs=[pl.BlockSpec((None, t, LANES), lambda i, g: (i, 0, g)),
                  pl.BlockSpec((GRID_W, 2 * n1, n1), lambda i, g: (0, 0, 0)),
                  pl.BlockSpec((2 * GRID_W, 2 * GRID_W), lambda i, g: (0, 0)),
                  pl.BlockSpec((2 * HEAD_W, HEAD_W), lambda i, g: (0, 0))],
        out_specs=pl.BlockSpec((None, t, LANES), lambda i, g: (i, 0, g)),
        out_shape=jax.ShapeDtypeStruct((b, t, w), F32),
        scratch_shapes=[pltpu.VMEM((GRID_W * pitch, LANES), F32), pltpu.VMEM((GRID_W * pitch, LANES), F32)],
        compiler_params=_cparams("arbitrary", "arbitrary"),
        name="dft2",
    )(f, ftw, f2, f3)


def _outproj_kernel(fo_ref, of_ref, ob_ref, z_ref, x_ref, gate_ref, shift_ref, scale_ref,
                    og_ref, n2_ref, wo_ref, wrh_ref, wrl_ref, br_ref, x1_ref, h2_ref, rt_ref):
    o = of_ref[...].astype(F32) + ob_ref[...].astype(F32)
    z = z_ref[...].astype(F32)
    parts = [fo_ref[...].astype(BF16)]
    for h in range(DN_HEADS):
        hs = slice(h * HEAD_W, (h + 1) * HEAD_W)
        oh = o[:, hs]
        ms = jnp.mean(oh * oh, axis=-1, keepdims=True)
        on = oh * lax.rsqrt(ms + EPS) * og_ref[...]
        parts.append((on * _silu(z[:, hs])).astype(BF16))
    mix = jnp.concatenate(parts, axis=1)
    y = _dot(mix, wo_ref[...])
    x1 = x_ref[...] + gate_ref[0] * y
    x1_ref[...] = x1
    ms = jnp.mean(x1 * x1, axis=-1, keepdims=True)
    h2 = x1 * lax.rsqrt(ms + EPS) * n2_ref[...]
    h2 = h2 * (1.0 + scale_ref[0]) + shift_ref[0]
    h2_ref[...] = h2
    hh = h2.astype(BF16)
    hl = (h2 - hh.astype(F32)).astype(BF16)
    lg = _dot(hh, wrh_ref[...]) + _dot(hl, wrh_ref[...]) + _dot(hh, wrl_ref[...]) + br_ref[...]
    lane = lax.broadcasted_iota(jnp.int32, lg.shape, 1)
    lane_f = lane.astype(F32)
    neg = jnp.float32(-3.0e38)
    big = jnp.float32(1.0e6)
    is_g = lane < N_GROUPS
    mg = jnp.max(jnp.where(is_g, lg, neg), axis=-1, keepdims=True)
    sg = jnp.sum(jnp.where(is_g, jnp.exp(jnp.where(is_g, lg - mg, 0.0)), 0.0), axis=-1, keepdims=True)
    g_top = jnp.min(jnp.where(is_g & (lg == mg), lane_f, big), axis=-1, keepdims=True)
    pg_top = 1.0 / sg
    eidx = lane - N_GROUPS
    eidx_f = eidx.astype(F32)
    grp_f = (eidx >> 3).astype(F32)
    in_grp = (eidx >= 0) & (eidx < N_EXPERTS) & (grp_f == g_top)
    m1 = jnp.max(jnp.where(in_grp, lg, neg), axis=-1, keepdims=True)
    i1 = jnp.min(jnp.where(in_grp & (lg == m1), eidx_f, big), axis=-1, keepdims=True)
    rest = in_grp & (eidx_f != i1)
    m2 = jnp.max(jnp.where(rest, lg, neg), axis=-1, keepdims=True)
    i2 = jnp.min(jnp.where(rest & (lg == m2), eidx_f, big), axis=-1, keepdims=True)
    e2 = jnp.exp(m2 - m1)
    w0 = pg_top / (1.0 + e2)
    w1 = pg_top * e2 / (1.0 + e2)
    rt_ref[...] = jnp.where(lane == 0, i1,
                            jnp.where(lane == 1, i2,
                                      jnp.where(lane == 2, w0, jnp.where(lane == 3, w1, 0.0))))


def _outproj(fo, of, ob, z, x2, gate1, shift2, scale2, onorm_g, norm2_g, w_out, w_group, b_group,
             w_router, b_router, tokens_per_batch):
    n, d = x2.shape
    tm = TM_OUT
    w = fo.shape[1]
    wr = jnp.pad(jnp.concatenate([w_group, w_router], axis=1), ((0, 0), (0, LANES - N_GROUPS - N_EXPERTS)))
    wrh = wr.astype(BF16)
    wrl = (wr - wrh.astype(F32)).astype(BF16)
    br = jnp.pad(jnp.concatenate([b_group, b_router]).reshape(1, -1), ((0, 0), (0, LANES - N_GROUPS - N_EXPERTS)))
    steps_per_batch = tokens_per_batch // tm
    bmap = lambda i: (i // steps_per_batch, 0, 0)
    const = lambda i: (0, 0)
    row = lambda i: (i, 0)
    return pl.pallas_call(
        _outproj_kernel,
        grid=(n // tm,),
        in_specs=[pl.BlockSpec((tm, w), row), pl.BlockSpec((tm, w), row), pl.BlockSpec((tm, w), row),
                  pl.BlockSpec((tm, w), row), pl.BlockSpec((tm, d), row),
                  pl.BlockSpec((1, 1, d), bmap), pl.BlockSpec((1, 1, d), bmap), pl.BlockSpec((1, 1, d), bmap),
                  pl.BlockSpec((1, HEAD_W), const), pl.BlockSpec((1, d), const),
                  pl.BlockSpec((d, d), const), pl.BlockSpec((d, LANES), const), pl.BlockSpec((d, LANES), const),
                  pl.BlockSpec((1, LANES), const)],
        out_specs=[pl.BlockSpec((tm, d), row), pl.BlockSpec((tm, d), row), pl.BlockSpec((tm, LANES), row)],
        out_shape=[jax.ShapeDtypeStruct((n, d), F32), jax.ShapeDtypeStruct((n, d), F32),
                   jax.ShapeDtypeStruct((n, LANES), F32)],
        compiler_params=_cparams("arbitrary"),
        name="outproj_router",
    )(fo, of, ob, z, x2, gate1, shift2, scale2, onorm_g.reshape(1, HEAD_W), norm2_g.reshape(1, d),
      w_out.astype(BF16), wrh, wrl, br)


def _rank_kernel(rt_ref, rk_ref, cnt_ref, carry_ref):
    i = pl.program_id(0)

    @pl.when(i == 0)
    def _():
        carry_ref[...] = jnp.zeros_like(carry_ref)

    rt = rt_ref[...]
    tm = rt.shape[0]
    lane = lax.broadcasted_iota(jnp.int32, rt.shape, 1)
    e0 = rt[:, 0:1].astype(jnp.int32)
    e1 = rt[:, 1:2].astype(jnp.int32)
    oh0 = (lane == e0).astype(F32)
    oh1 = (lane == e1).astype(F32)
    both = oh0 + oh1
    r = lax.broadcasted_iota(jnp.int32, (tm, tm), 0)
    c = lax.broadcasted_iota(jnp.int32, (tm, tm), 1)
    lower = (r > c).astype(BF16)
    before = _dot(lower, both.astype(BF16)) + carry_ref[0:1, :]
    rank0 = jnp.sum(before * oh0, axis=-1, keepdims=True)
    rank1 = jnp.sum(before * oh1, axis=-1, keepdims=True)
    rk_ref[...] = jnp.where(lane == 0, rank0, jnp.where(lane == 1, rank1, 0.0))
    total = carry_ref[0:1, :] + jnp.sum(both, axis=0, keepdims=True)
    carry_ref[...] = jnp.broadcast_to(total, carry_ref.shape)
    cnt_ref[...] = jnp.broadcast_to(total, cnt_ref.shape)


def _rank(rt):
    n = rt.shape[0]
    tm = min(TM_RANK, n)
    return pl.pallas_call(
        _rank_kernel,
        grid=(n // tm,),
        in_specs=[pl.BlockSpec((tm, LANES), lambda i: (i, 0))],
        out_specs=[pl.BlockSpec((tm, LANES), lambda i: (i, 0)),
                   pl.BlockSpec((SUBLANES, LANES), lambda i: (0, 0))],
        out_shape=[jax.ShapeDtypeStruct((n, LANES), F32), jax.ShapeDtypeStruct((SUBLANES, LANES), F32)],
        scratch_shapes=[pltpu.VMEM((SUBLANES, LANES), F32)],
        compiler_params=_cparams("arbitrary"),
        name="moe_rank",
    )(rt)


def _dest_kernel(rt_ref, rk_ref, ps_ref, d_ref):
    rt = rt_ref[...]
    rk = rk_ref[...]
    lane = lax.broadcasted_iota(jnp.int32, rt.shape, 1)
    e0 = rt[:, 0:1].astype(jnp.int32)
    e1 = rt[:, 1:2].astype(jnp.int32)
    ps = ps_ref[0:1, :]
    d0 = jnp.sum(jnp.where(lane == e0, ps, 0.0), axis=-1, keepdims=True) + rk[:, 0:1]
    d1 = jnp.sum(jnp.where(lane == e1, ps, 0.0), axis=-1, keepdims=True) + rk[:, 1:2]
    d_ref[...] = jnp.where(lane == 0, d0, jnp.where(lane == 1, d1, 0.0)).astype(jnp.int32)


def _dest(rt, rk, pstart_row):
    n = rt.shape[0]
    tm = min(TM_RANK, n)
    return pl.pallas_call(
        _dest_kernel,
        grid=(n // tm,),
        in_specs=[pl.BlockSpec((tm, LANES), lambda i: (i, 0)),
                  pl.BlockSpec((tm, LANES), lambda i: (i, 0)),
                  pl.BlockSpec((SUBLANES, LANES), lambda i: (0, 0))],
        out_specs=pl.BlockSpec((tm, LANES), lambda i: (i, 0)),
        out_shape=jax.ShapeDtypeStruct((n, LANES), jnp.int32),
        compiler_params=_cparams("arbitrary"),
        name="moe_dest",
    )(rt, rk, pstart_row)


def _dispatch_kernel(pend_ref, na_ref, d0_ref, d1_ref, h_ref, xs_ref, zbuf, sem):
    tm = h_ref.shape[0]
    n_blocks = xs_ref.shape[0] // MOE_TILE

    @pl.when(pl.program_id(0) == 0)
    def _():
        zbuf[...] = jnp.zeros_like(zbuf)

        def zero_tile(tile):
            return pltpu.make_async_copy(zbuf, xs_ref.at[pl.ds(pl.multiple_of(tile * MOE_TILE, MOE_TILE), MOE_TILE)], sem)

        def last_tile(e):
            prev_end = jnp.where(e == 0, 0, pend_ref[jnp.maximum(e - 1, 0)])
            return pend_ref[e] > prev_end, pend_ref[e] // MOE_TILE - 1

        def start_e(e, carry):
            nonempty, tile = last_tile(e)

            @pl.when(nonempty)
            def _():
                zero_tile(tile).start()
            return carry

        def wait_e(e, carry):
            nonempty, tile = last_tile(e)

            @pl.when(nonempty)
            def _():
                zero_tile(tile).wait()
            return carry

        def start_t(tile, carry):
            zero_tile(tile).start()
            return carry

        def wait_t(tile, carry):
            zero_tile(tile).wait()
            return carry

        lax.fori_loop(0, N_EXPERTS, start_e, 0)
        lax.fori_loop(na_ref[0], n_blocks, start_t, 0)
        lax.fori_loop(0, N_EXPERTS, wait_e, 0)
        lax.fori_loop(na_ref[0], n_blocks, wait_t, 0)

    def issue(g, carry):
        for u in range(ROW_UNROLL):
            t = g * ROW_UNROLL + u
            pltpu.make_async_copy(h_ref.at[pl.ds(t, 1)], xs_ref.at[pl.ds(d0_ref[0, 0, t], 1)], sem).start()
            pltpu.make_async_copy(h_ref.at[pl.ds(t, 1)], xs_ref.at[pl.ds(d1_ref[0, 0, t], 1)], sem).start()
        return carry

    lax.fori_loop(0, tm // ROW_UNROLL, issue, 0)

    def drain(g, carry):
        for _ in range(2 * ROW_UNROLL):
            pltpu.make_async_copy(h_ref.at[pl.ds(0, 1)], xs_ref.at[pl.ds(0, 1)], sem).wait()
        return carry

    lax.fori_loop(0, tm // ROW_UNROLL, drain, 0)


def _dispatch(h2, d0, d1, pend, n_active, n_slots):
    n, d = h2.shape
    tm = min(TM_ROW, n)
    smem = lambda: pl.BlockSpec((1, 1, tm), lambda i, pe, na: (i, 0, 0), memory_space=pltpu.SMEM)
    grid_spec = pltpu.PrefetchScalarGridSpec(
        num_scalar_prefetch=2,
        grid=(n // tm,),
        in_specs=[smem(), smem(), pl.BlockSpec((tm, d), lambda i, pe, na: (i, 0))],
        out_specs=pl.BlockSpec(memory_space=pl.ANY),
        scratch_shapes=[pltpu.VMEM((MOE_TILE, d), F32), pltpu.SemaphoreType.DMA(())],
    )
    return pl.pallas_call(
        _dispatch_kernel,
        grid_spec=grid_spec,
        out_shape=jax.ShapeDtypeStruct((n_slots, d), F32),
        compiler_params=_cparams("arbitrary"),
        name="moe_dispatch",
    )(pend, n_active, d0.reshape(n // tm, 1, tm), d1.reshape(n // tm, 1, tm), h2)


def _expert_kernel(be_ref, na_ref, xs_ref, wg_ref, wu_ref, wd_ref, ys_ref, wgb, wub, wdb):
    i = pl.program_id(0)
    prev = be_ref[jnp.maximum(i - 1, 0)]
    fresh = (i == 0) | (be_ref[i] != prev)
    active = i < na_ref[0]

    @pl.when(active & fresh)
    def _():
        wgb[...] = wg_ref[0].astype(BF16)
        wub[...] = wu_ref[0].astype(BF16)
        wdb[...] = wd_ref[0].astype(BF16)

    @pl.when(active)
    def _():
        x = xs_ref[...].astype(BF16)
        hid = _silu(_dot(x, wgb[...])) * _dot(x, wub[...])
        ys_ref[...] = _dot(hid.astype(BF16), wdb[...])

    @pl.when(jnp.logical_not(active))
    def _():
        ys_ref[...] = jnp.zeros_like(ys_ref)


def _experts(xs, blk_expert, n_active, w_gate, w_up, w_down):
    n_slots, d = xs.shape
    n_blocks = n_slots // MOE_TILE
    de = w_gate.shape[2]
    xmap = lambda i, be, na: (jnp.minimum(i, na[0] - 1), 0)
    wmap = lambda i, be, na: (be[i], 0, 0)
    grid_spec = pltpu.PrefetchScalarGridSpec(
        num_scalar_prefetch=2,
        grid=(n_blocks,),
        in_specs=[pl.BlockSpec((MOE_TILE, d), xmap),
                  pl.BlockSpec((1, d, de), wmap),
                  pl.BlockSpec((1, d, de), wmap),
                  pl.BlockSpec((1, de, d), wmap)],
        out_specs=pl.BlockSpec((MOE_TILE, d), lambda i, be, na: (i, 0)),
        scratch_shapes=[pltpu.VMEM((d, de), BF16), pltpu.VMEM((d, de), BF16), pltpu.VMEM((de, d), BF16)],
    )
    return pl.pallas_call(
        _expert_kernel,
        grid_spec=grid_spec,
        out_shape=jax.ShapeDtypeStruct((n_slots, d), F32),
        compiler_params=_cparams("arbitrary"),
        name="moe_experts",
    )(blk_expert, n_active, xs, w_gate, w_up, w_down)


def _combine_kernel(d0c_ref, d1c_ref, d0n_ref, d1n_ref, ys_ref, rt_ref, x1_ref, gate_ref, fg_ref, o_ref,
                    ya, yb, sem):
    tm = x1_ref.shape[0]
    i = pl.program_id(0)
    slot = i % 2

    def gather_block(d0_ref, d1_ref, s):
        def issue(g, carry):
            for u in range(ROW_UNROLL):
                t = g * ROW_UNROLL + u
                pltpu.make_async_copy(ys_ref.at[pl.ds(d0_ref[0, 0, t], 1)], ya.at[s, pl.ds(t, 1)],
                                      sem.at[s]).start()
                pltpu.make_async_copy(ys_ref.at[pl.ds(d1_ref[0, 0, t], 1)], yb.at[s, pl.ds(t, 1)],
                                      sem.at[s]).start()
            return carry

        lax.fori_loop(0, tm // ROW_UNROLL, issue, 0)

    @pl.when(i == 0)
    def _():
        gather_block(d0c_ref, d1c_ref, 0)

    @pl.when(i + 1 < pl.num_programs(0))
    def _():
        gather_block(d0n_ref, d1n_ref, 1 - slot)

    def drain(g, carry):
        for _ in range(ROW_UNROLL):
            pltpu.make_async_copy(ys_ref.at[pl.ds(0, 1)], ya.at[slot, pl.ds(0, 1)], sem.at[slot]).wait()
            pltpu.make_async_copy(ys_ref.at[pl.ds(0, 1)], yb.at[slot, pl.ds(0, 1)], sem.at[slot]).wait()
        return carry

    lax.fori_loop(0, tm // ROW_UNROLL, drain, 0)
    rt = rt_ref[...]
    moe = rt[:, 2:3] * ya[slot] + rt[:, 3:4] * yb[slot]
    xo = x1_ref[...] + gate_ref[0] * moe
    ms = jnp.mean(xo * xo, axis=-1, keepdims=True)
    o_ref[...] = xo * lax.rsqrt(ms + EPS) * fg_ref[...]


def _combine(ys, d0, d1, rt, x1, gate2, final_g, tokens_per_batch):
    n, d = x1.shape
    tm = min(TM_ROW, tokens_per_batch)
    steps_per_batch = tokens_per_batch // tm
    steps = n // tm
    cur = lambda: pl.BlockSpec((1, 1, tm), lambda i: (i, 0, 0), memory_space=pltpu.SMEM)
    nxt = lambda: pl.BlockSpec((1, 1, tm), lambda i: (jnp.minimum(i + 1, steps - 1), 0, 0),
                               memory_space=pltpu.SMEM)
    d0r = d0.reshape(steps, 1, tm)
    d1r = d1.reshape(steps, 1, tm)
    return pl.pallas_call(
        _combine_kernel,
        grid=(steps,),
        in_specs=[cur(), cur(), nxt(), nxt(),
                  pl.BlockSpec(memory_space=pl.ANY),
                  pl.BlockSpec((tm, LANES), lambda i: (i, 0)),
                  pl.BlockSpec((tm, d), lambda i: (i, 0)),
                  pl.BlockSpec((1, 1, d), lambda i: (i // steps_per_batch, 0, 0)),
                  pl.BlockSpec((1, d), lambda i: (0, 0))],
        out_specs=pl.BlockSpec((tm, d), lambda i: (i, 0)),
        out_shape=jax.ShapeDtypeStruct((n, d), F32),
        scratch_shapes=[pltpu.VMEM((2, tm, d), F32), pltpu.VMEM((2, tm, d), F32),
                        pltpu.SemaphoreType.DMA((2,))],
        compiler_params=_cparams("arbitrary"),
        name="moe_combine",
    )(d0r, d1r, d0r, d1r, ys, rt, x1, gate2, final_g.reshape(1, d))


def _moe_and_final(h2, rt, x1, gate2, final_g, w_gate, w_up, w_down, tokens_per_batch):
    n, d = h2.shape
    rk, cnt = _rank(rt)
    counts = cnt[0, :N_EXPERTS].astype(jnp.int32)
    pcounts = (counts + MOE_TILE - 1) // MOE_TILE * MOE_TILE
    pend = jnp.cumsum(pcounts)
    pstart = pend - pcounts
    n_slots = -(-(2 * n) // MOE_TILE) * MOE_TILE + N_EXPERTS * MOE_TILE
    n_blocks = n_slots // MOE_TILE
    blk_start = jnp.arange(n_blocks, dtype=jnp.int32) * MOE_TILE
    blk_expert = jnp.minimum(jnp.sum((pend[None, :] <= blk_start[:, None]).astype(jnp.int32), axis=1),
                             N_EXPERTS - 1)
    n_active = (pend[-1:] // MOE_TILE).astype(jnp.int32)
    ps_row = jnp.broadcast_to(jnp.pad(pstart.astype(F32), (0, LANES - N_EXPERTS))[None, :], (SUBLANES, LANES))
    dest = _dest(rt, rk, ps_row)
    d0 = dest[:, 0]
    d1 = dest[:, 1]
    xs = _dispatch(h2, d0, d1, pend.astype(jnp.int32), n_active, n_slots)
    ys = _experts(xs, blk_expert, n_active, w_gate, w_up, w_down)
    return _combine(ys, d0, d1, rt, x1, gate2, final_g, tokens_per_batch)


def _layer(x, c, ctx, c_ctx, w_mod, b_mod, norm1_g, w_in, conv_w, a_log, dt_bias, onorm_g, w_out,
           norm2_g, w_group, b_group, w_router, b_router, w_gate, w_up, w_down, final_g):
    b, t, d = x.shape
    tc = ctx.shape[1]
    rows = -(-(b + 1) // SUBLANES) * SUBLANES
    cc = jnp.zeros((rows, d), F32).at[:b].set(c).at[b].set(c_ctx)
    mod = _adaln(cc, w_mod, b_mod)
    mx = [mod[:b, i * d:(i + 1) * d].reshape(b, 1, d) for i in range(6)]
    mc = [jnp.broadcast_to(mod[b:b + 1, i * d:(i + 1) * d].reshape(1, 1, d), (b, 1, d)) for i in range(2)]

    _, qkv_c, _, gb_c = _inproj(ctx.reshape(b * tc, d), mc[0], mc[1], norm1_g, w_in, a_log, dt_bias, tc)
    feat_c = _conv_features(qkv_c.reshape(b, tc, -1), conv_w, grid_mode=False)
    gb_c = gb_c.reshape(b, tc, LANES)
    zero_state = jnp.zeros((b, DN_HEADS, HEAD_W, HEAD_W), F32)
    _, s_fwd = _delta_scan(feat_c, gb_c, zero_state, reverse=False)
    _, s_bwd = _delta_scan(feat_c, gb_c, zero_state, reverse=True)

    x2 = x.reshape(b * t, d)
    f, qkv, z, gb = _inproj(x2, mx[0], mx[1], norm1_g, w_in, a_log, dt_bias, t)
    feat = _conv_features(qkv.reshape(b, t, -1), conv_w, grid_mode=True)
    gb = gb.reshape(b, t, LANES)
    o_f, _ = _delta_scan(feat, gb, s_fwd, reverse=False)
    o_b, _ = _delta_scan(feat, gb, s_bwd, reverse=True)
    fo = _fourier_mix(f.reshape(b, t, -1))
    x1, h2, rt = _outproj(fo.reshape(b * t, -1), o_f.reshape(b * t, -1), o_b.reshape(b * t, -1), z, x2,
                          mx[2], mx[3], mx[4], onorm_g, norm2_g, w_out, w_group, b_group, w_router,
                          b_router, t)
    out = _moe_and_final(h2, rt, x1, mx[5], final_g, w_gate, w_up, w_down, t)
    return out.reshape(b, t, d)


def kernel(x, c, ctx, c_ctx, w_mod, b_mod, norm1_g, w_in, conv_w, a_log, dt_bias, onorm_g, w_out, norm2_g,
           w_group, b_group, w_router, b_router, w_gate, w_up, w_down, final_g):
    assert w_mod.shape[0] == 1, "single-layer trunk"
    return _layer(x, c, ctx, c_ctx, w_mod[0], b_mod[0], norm1_g[0], w_in[0], conv_w[0], a_log[0], dt_bias[0],
                  onorm_g[0], w_out[0], norm2_g[0], w_group[0], b_group[0], w_router[0], b_router[0],
                  w_gate[0], w_up[0], w_down[0], final_g)
```

```python
import functools
import math

import numpy as np
import jax
import jax.numpy as jnp
from jax import lax
from jax.experimental import pallas as pl
from jax.experimental.pallas import tpu as pltpu

F32 = jnp.float32
BF16 = jnp.bfloat16

GRID_W = 64
F_GROUPS = 4
DN_HEADS = 4
HEAD_W = 128
CHUNK = 64
N_GROUPS = 4
EXPERTS_PER_GROUP = 8
N_EXPERTS = N_GROUPS * EXPERTS_PER_GROUP
EPS = 1e-6

LANES = 128
SUBLANES = 8
VMEM_LIMIT = 56 * 1024 * 1024

TM_IN = 1024
TM_OUT = 512
TM_RANK = 1024
TM_DEST = 2048
TM_ROW = 256
MOE_TILE = 512
ROW_UNROLL = 8
FFT_TB = 8
CONV_ROWS = 256
CONV_PAD = 72


def _cparams(*sem):
    return pltpu.CompilerParams(dimension_semantics=sem, vmem_limit_bytes=VMEM_LIMIT)


def _silu(v):
    return v * jax.nn.sigmoid(v)


def _dot(a, b):
    return jnp.dot(a, b, preferred_element_type=F32)


ROW_TILE = SUBLANES


def _store_token_tiles(ref, val, row0=0):
    rows = val.shape[0]
    for j in range(val.shape[1] // LANES):
        ref[pl.ds(row0 * ROW_TILE + j, rows, stride=ROW_TILE), :] = val[:, j * LANES:(j + 1) * LANES]


def _load_token_tiles(ref, rows, row0=0, width=ROW_TILE * LANES):
    return jnp.concatenate(
        [ref[pl.ds(row0 * ROW_TILE + j, rows, stride=ROW_TILE), :] for j in range(width // LANES)], axis=1)


def _adaln_kernel(c_ref, w_ref, b_ref, o_ref):
    a = _silu(c_ref[...])
    o_ref[...] = jnp.dot(a, w_ref[...], preferred_element_type=F32,
                         precision=lax.Precision.HIGHEST) + b_ref[...]


def _adaln(cc, w_mod, b_mod):
    rows, d = cc.shape
    n = w_mod.shape[1]
    tn = 1024
    return pl.pallas_call(
        _adaln_kernel,
        grid=(n // tn,),
        in_specs=[pl.BlockSpec((rows, d), lambda j: (0, 0)),
                  pl.BlockSpec((d, tn), lambda j: (0, j)),
                  pl.BlockSpec((1, tn), lambda j: (0, j))],
        out_specs=pl.BlockSpec((rows, tn), lambda j: (0, j)),
        out_shape=jax.ShapeDtypeStruct((rows, n), F32),
        compiler_params=_cparams("arbitrary"),
        name="adaln",
    )(cc, w_mod, b_mod.reshape(1, n))


def _modulated_norm(x, g, shift, scale):
    ms = jnp.mean(x * x, axis=-1, keepdims=True)
    h = x * lax.rsqrt(ms + EPS) * g
    return (h * (1.0 + scale) + shift).astype(BF16)


def _gate_features(gates, alog_ref, dtb_ref):
    tm = gates.shape[0]
    lane = lax.broadcasted_iota(jnp.int32, gates.shape, 1)
    pos = lax.broadcasted_iota(jnp.int32, gates.shape, 0) & (CHUNK - 1)
    beta = jax.nn.sigmoid(gates)
    sp_in = gates + dtb_ref[...]
    softplus = jnp.maximum(sp_in, 0.0) + jnp.log1p(jnp.exp(-jnp.abs(sp_in)))
    g = -jnp.exp(alog_ref[...]) * softplus
    g = jnp.where((lane >= 8) & (lane < 16), g, 0.0)
    pre = g
    suf = g
    s = 1
    while s < CHUNK:
        pre = pre + jnp.where(pos >= s, pltpu.roll(pre, s, 0), 0.0)
        suf = suf + jnp.where(pos < CHUNK - s, pltpu.roll(suf, tm - s, 0), 0.0)
        s *= 2
    return jnp.where(lane < 8, beta, jnp.where(lane < 12, pre, suf))


def _inproj_kernel(x_ref, shift_ref, scale_ref, g_ref, wf_ref, wqkv_ref, wz_ref, wg_ref,
                   alog_ref, dtb_ref, f_ref, qkv_ref, z_ref, gb_ref, gt_ref):
    hb = _modulated_norm(x_ref[...], g_ref[...], shift_ref[0], scale_ref[0])
    f_ref[...] = _dot(hb, wf_ref[...])
    qkv_ref[...] = _dot(hb, wqkv_ref[...])
    z_ref[...] = _dot(hb, wz_ref[...]).astype(BF16)
    gb = _gate_features(_dot(hb, wg_ref[...]), alog_ref, dtb_ref)
    gb_ref[...] = gb
    gt_ref[...] = gb.T[8:16, :]


def _split_w_in(w_in, a_log, dt_bias):
    f_w = F_GROUPS * HEAD_W
    qkv_w = 3 * DN_HEADS * HEAD_W
    z_w = DN_HEADS * HEAD_W
    wb = w_in.astype(BF16)
    wg = jnp.pad(wb[:, f_w + qkv_w + z_w:], ((0, 0), (0, LANES - 4 * DN_HEADS)))
    alog = jnp.pad(a_log.reshape(1, -1), ((0, 0), (8, LANES - 16)))
    dtb = jnp.pad(dt_bias.reshape(1, -1), ((0, 0), (8, LANES - 16)))
    return wb[:, :f_w], wb[:, f_w:f_w + qkv_w], wb[:, f_w + qkv_w:f_w + qkv_w + z_w], wg, alog, dtb


def _inproj(x2, shift, scale, norm_g, w_in, a_log, dt_bias, tokens_per_batch):
    n, d = x2.shape
    tm = min(TM_IN, tokens_per_batch)
    wf, wqkv, wz, wg, alog, dtb = _split_w_in(w_in, a_log, dt_bias)
    f_w, qkv_w, z_w = wf.shape[1], wqkv.shape[1], wz.shape[1]
    steps_per_batch = tokens_per_batch // tm
    bmap = lambda i: (i // steps_per_batch, 0, 0)
    const = lambda i: (0, 0)
    row = lambda i: (i, 0)
    return pl.pallas_call(
        _inproj_kernel,
        grid=(n // tm,),
        in_specs=[pl.BlockSpec((tm, d), row),
                  pl.BlockSpec((1, 1, d), bmap),
                  pl.BlockSpec((1, 1, d), bmap),
                  pl.BlockSpec((1, d), const),
                  pl.BlockSpec((d, f_w), const),
                  pl.BlockSpec((d, qkv_w), const),
                  pl.BlockSpec((d, z_w), const),
                  pl.BlockSpec((d, LANES), const),
                  pl.BlockSpec((1, LANES), const),
                  pl.BlockSpec((1, LANES), const)],
        out_specs=[pl.BlockSpec((tm, f_w), row),
                   pl.BlockSpec((tm, qkv_w), row),
                   pl.BlockSpec((tm, z_w), row),
                   pl.BlockSpec((tm, LANES), row),
                   pl.BlockSpec((SUBLANES, tm), lambda i: (0, i))],
        out_shape=[jax.ShapeDtypeStruct((n, f_w), F32),
                   jax.ShapeDtypeStruct((n, qkv_w), F32),
                   jax.ShapeDtypeStruct((n, z_w), BF16),
                   jax.ShapeDtypeStruct((n, LANES), F32),
                   jax.ShapeDtypeStruct((SUBLANES, n), F32)],
        compiler_params=_cparams("arbitrary"),
        name="inproj",
    )(x2, shift, scale, norm_g.reshape(1, d), wf, wqkv, wz, wg, alog, dtb)


def _conv_rows(src, row0, lanes, w, rows, grid_mode):
    col = lax.broadcasted_iota(jnp.int32, (rows, LANES), 0) & (GRID_W - 1)
    acc = None
    for dc in (0, -1, 1):
        part = None
        for dr in ((-1, 0, 1) if grid_mode else (0,)):
            off = row0 + GRID_W * dr + dc
            tap = 3 * (dr + 1) + (dc + 1)
            term = src[off:off + rows, lanes] * w[tap:tap + 1, :]
            part = term if part is None else part + term
        if grid_mode and dc == -1:
            part = jnp.where(col != 0, part, 0.0)
        if grid_mode and dc == 1:
            part = jnp.where(col != GRID_W - 1, part, 0.0)
        acc = part if acc is None else acc + part
    return _silu(acc)


def _conv_kernel(x_ref, w_ref, o_ref, pad_ref, *, grid_mode):
    t = x_ref.shape[1]
    j = pl.program_id(1)
    zeros = jnp.zeros((CONV_PAD, LANES), F32)
    pad_ref[0:CONV_PAD, :] = zeros
    pad_ref[CONV_PAD + t:CONV_PAD + t + CONV_PAD, :] = zeros
    pad_ref[CONV_PAD:CONV_PAD + t, :] = x_ref[0]
    rows = min(CONV_ROWS, t)
    w = w_ref[...]
    for r0 in range(0, t, rows):
        y = _conv_rows(pad_ref, CONV_PAD + r0, slice(None), w, rows, grid_mode)
        inv = lax.rsqrt(jnp.sum(y * y, axis=-1, keepdims=True) + EPS)
        fac = jnp.where(j < DN_HEADS, inv * (HEAD_W ** -0.5), jnp.where(j < 2 * DN_HEADS, inv, 1.0))
        o_ref[0, r0:r0 + rows, :] = y * fac


def _conv_features(qkv, conv_w, grid_mode):
    b, t, ch = qkv.shape
    w9 = conv_w.reshape(9, ch)
    return pl.pallas_call(
        functools.partial(_conv_kernel, grid_mode=grid_mode),
        grid=(b, ch // LANES),
        in_specs=[pl.BlockSpec((1, t, LANES), lambda i, j: (i, 0, j)),
                  pl.BlockSpec((9, LANES), lambda i, j: (0, j))],
        out_specs=pl.BlockSpec((1, t, LANES), lambda i, j: (i, 0, j)),
        out_shape=jax.ShapeDtypeStruct((b, t, ch), F32),
        scratch_shapes=[pltpu.VMEM((t + 2 * CONV_PAD, LANES), F32)],
        compiler_params=_cparams("arbitrary", "arbitrary"),
        name="conv_grid" if grid_mode else "conv_seq",
    )(qkv, w9)


def _delta_kernel(q_ref, k_ref, v_ref, gb_ref, grow_ref, s0_ref, o_ref, sfin_ref, s_scr, *,
                  reverse, cb):
    j = pl.program_id(1)
    nh = DN_HEADS
    rr = nh * CHUNK

    @pl.when(j == 0)
    def _():
        s_scr[...] = s0_ref[0]

    lane_b = nh if reverse else 0
    lane_g = 8 + (nh if reverse else 0)
    ri = lax.broadcasted_iota(jnp.int32, (rr, rr), 0)
    ci = lax.broadcasted_iota(jnp.int32, (rr, rr), 1)
    same = (ri >> 6) == (ci >> 6)
    if reverse:
        tri = same & (ri <= ci)
        strict = same & (ri < ci)
    else:
        tri = same & (ri >= ci)
        strict = same & (ri > ci)
    eye = (ri == ci).astype(F32)

    order = list(range(cb - 1, -1, -1) if reverse else range(cb))

    prep = []
    for c in order:
        rows = slice(c * CHUNK, (c + 1) * CHUNK)
        q = q_ref[0, rows, :]
        k = k_ref[0, rows, :]
        v = v_ref[0, rows, :]
        gb = gb_ref[0, rows, :]
        grow = grow_ref[0, c:c + 1, :]
        kb_l, qc_l, kc_l, vb_l, kbe_l, qg_l, kd_l, gcb_l, egl_l = [], [], [], [], [], [], [], [], []
        for h in range(nh):
            hs = slice(h * HEAD_W, (h + 1) * HEAD_W)
            beta = jnp.broadcast_to(gb[:, lane_b + h:lane_b + h + 1], (CHUNK, HEAD_W))
            gc = jnp.broadcast_to(gb[:, lane_g + h:lane_g + h + 1], (CHUNK, HEAD_W))
            glast = gc[0:1, :] if reverse else gc[CHUNK - 1:CHUNK, :]
            eg = jnp.exp(gc)
            kh = k[:, hs]
            kbh = kh * beta
            kb_l.append(kbh)
            kc_l.append(kh)
            qc_l.append(q[:, hs])
            vb_l.append(v[:, hs] * beta)
            kbe_l.append(kbh * eg)
            qg_l.append(q[:, hs] * eg)
            kd_l.append(kh * jnp.exp(glast - gc))
            gcb_l.append(gc)
            egl_l.append(jnp.exp(glast))
        kc = jnp.concatenate(kc_l, axis=0).astype(BF16)
        lhs = jnp.concatenate(kb_l + qc_l, axis=0).astype(BF16)
        a = lax.dot_general(lhs, kc, (((1,), (1,)), ((), ())), preferred_element_type=F32)
        gcb = jnp.concatenate(gcb_l, axis=0)
        gcol = jnp.concatenate([gcb] * (rr // HEAD_W), axis=1)
        diff = gcol - grow
        dec = jnp.where(tri, jnp.exp(jnp.where(tri, diff, 0.0)), 0.0)
        nm = -jnp.where(strict, a[:rr] * dec, 0.0)
        intra = (a[rr:] * dec).astype(BF16)
        rhs = jnp.concatenate([jnp.concatenate(vb_l, axis=0), jnp.concatenate(kbe_l, axis=0)],
                              axis=1).astype(BF16)
        kdt_l = [kd.T.astype(BF16) for kd in kd_l]
        prep.append((rows, nm, intra, rhs, jnp.concatenate(qg_l, axis=0), kdt_l, egl_l))

    p_l = [eye + pr[1] for pr in prep]
    x_l = [pr[1].astype(BF16) for pr in prep]
    x_l = [_dot(xp, xp).astype(BF16) for xp in x_l]
    step = 2
    while step < CHUNK // 2:
        res = [_dot(jnp.concatenate([xp, p.astype(BF16)], axis=0), xp) for p, xp in zip(p_l, x_l)]
        p_l = [p + r[rr:] for p, r in zip(p_l, res)]
        x_l = [r[:rr].astype(BF16) for r in res]
        step *= 2
    p_l = [p + _dot(p.astype(BF16), xp) for p, xp in zip(p_l, x_l)]

    loc = []
    for (rows, _, intra, rhs, qg, kdt_l, egl_l), p in zip(prep, p_l):
        uwb = _dot(p.astype(BF16), rhs).astype(BF16)
        iw = _dot(intra, uwb)
        qe = (qg - iw[:, HEAD_W:]).astype(BF16)
        gq_l = [_dot(kdt_l[h], uwb[h * CHUNK:(h + 1) * CHUNK, :]) for h in range(nh)]
        loc.append((rows, iw[:, :HEAD_W], qe, gq_l, egl_l))

    for rows, o_loc, qe, gq_l, egl_l in loc:
        for h in range(nh):
            hr = slice(h * CHUNK, (h + 1) * CHUNK)
            s_h = s_scr[h]
            lhs = jnp.concatenate([gq_l[h][:, HEAD_W:].astype(BF16), qe[hr, :]], axis=0)
            rs = _dot(lhs, s_h.astype(BF16))
            o_ref[0, rows, h * HEAD_W:(h + 1) * HEAD_W] = (o_loc[hr, :] + rs[HEAD_W:]).astype(BF16)
            s_scr[h] = egl_l[h] * s_h - rs[:HEAD_W] + gq_l[h][:, :HEAD_W]

    @pl.when(j == pl.num_programs(1) - 1)
    def _():
        sfin_ref[0] = s_scr[...]


def _delta_scan(qkv, gb, gt, s0, reverse):
    b, t, _ = qkv.shape
    n_chunks = t // CHUNK
    cb = 8 if n_chunks % 8 == 0 else 4
    nb = n_chunks // cb
    row0 = DN_HEADS if reverse else 0
    grow = gt[row0:row0 + DN_HEADS].reshape(DN_HEADS, b, n_chunks, CHUNK)
    grow = grow.transpose(1, 2, 0, 3).reshape(b * nb, cb, DN_HEADS * CHUNK)
    w = DN_HEADS * HEAD_W
    blk = (lambda j: nb - 1 - j) if reverse else (lambda j: j)
    return pl.pallas_call(
        functools.partial(_delta_kernel, reverse=reverse, cb=cb),
        grid=(b, nb),
        in_specs=[pl.BlockSpec((1, cb * CHUNK, w), lambda i, j: (i, blk(j), 0)),
                  pl.BlockSpec((1, cb * CHUNK, w), lambda i, j: (i, blk(j), 1)),
                  pl.BlockSpec((1, cb * CHUNK, w), lambda i, j: (i, blk(j), 2)),
                  pl.BlockSpec((1, cb * CHUNK, LANES), lambda i, j: (i, blk(j), 0)),
                  pl.BlockSpec((1, cb, DN_HEADS * CHUNK), lambda i, j: (i * nb + blk(j), 0, 0)),
                  pl.BlockSpec((1, DN_HEADS, HEAD_W, HEAD_W), lambda i, j: (i, 0, 0, 0))],
        out_specs=[pl.BlockSpec((1, cb * CHUNK, w), lambda i, j: (i, blk(j), 0)),
                   pl.BlockSpec((1, DN_HEADS, HEAD_W, HEAD_W), lambda i, j: (i, 0, 0, 0))],
        out_shape=[jax.ShapeDtypeStruct((b, t, w), BF16),
                   jax.ShapeDtypeStruct((b, DN_HEADS, HEAD_W, HEAD_W), F32)],
        scratch_shapes=[pltpu.VMEM((DN_HEADS, HEAD_W, HEAD_W), F32)],
        compiler_params=_cparams("arbitrary", "arbitrary"),
        name="delta_bwd" if reverse else "delta_fwd",
    )(qkv, qkv, qkv, gb, grow, s0)


def _dft_tables(t):
    n1 = t // GRID_W
    t1 = np.arange(n1)
    t2 = np.arange(GRID_W)
    ang = 2.0 * np.pi * (np.outer(t1, t1)[None] / n1 + (t2[:, None, None] * t1[None, :, None]) / t)
    ftw = np.concatenate([np.cos(ang), -np.sin(ang)], axis=1)
    a2 = 2.0 * np.pi * np.outer(t2, t2) / GRID_W
    c2, s2 = np.cos(a2), np.sin(a2)
    f2 = np.block([[c2, s2], [-s2, c2]])
    ch = np.arange(HEAD_W)
    a3 = 2.0 * np.pi * np.outer(ch, ch) / HEAD_W
    f3 = np.concatenate([np.cos(a3), np.sin(a3)], axis=0) / math.sqrt(t * HEAD_W)
    return (jnp.asarray(ftw, BF16), jnp.asarray(f2, BF16), jnp.asarray(f3, BF16))


def _dft_kernel(x_ref, ftw_ref, f2_ref, f3_ref, o_ref, are_ref, aim_ref):
    t = x_ref.shape[0]
    n1 = t // GRID_W
    pitch = are_ref.shape[0] // GRID_W
    for c in range(GRID_W):
        xc = x_ref[pl.ds(c, n1, stride=GRID_W), :].astype(BF16)
        a = _dot(ftw_ref[c], xc)
        are_ref[c * pitch:c * pitch + n1, :] = a[:n1]
        aim_ref[c * pitch:c * pitch + n1, :] = a[n1:]
    tb = min(FFT_TB, n1)
    for k0 in range(0, n1, tb):
        r = jnp.concatenate(
            [jnp.concatenate([are_ref[pl.ds(k0 + kk, GRID_W, stride=pitch), :],
                              aim_ref[pl.ds(k0 + kk, GRID_W, stride=pitch), :]], axis=0).astype(BF16)
             for kk in range(tb)], axis=1)
        g = _dot(f2_ref[...], r)
        gc = jnp.concatenate(
            [jnp.concatenate([g[:GRID_W, kk * LANES:(kk + 1) * LANES], g[GRID_W:, kk * LANES:(kk + 1) * LANES]],
                             axis=1) for kk in range(tb)], axis=0).astype(BF16)
        y = _dot(gc, f3_ref[...])
        for kk in range(tb):
            o_ref[pl.ds(k0 + kk, GRID_W, stride=n1), :] = y[kk * GRID_W:(kk + 1) * GRID_W]


def _fourier_mix(f):
    b, t, w = f.shape
    n1 = t // GRID_W
    ftw, f2, f3 = _dft_tables(t)
    pitch = n1 + SUBLANES
    return pl.pallas_call(
        _dft_kernel,
        grid=(b, w // LANES),
        in_specs=[pl.BlockSpec((None, t, LANES), lambda i, g: (i, 0, g)),
                  pl.BlockSpec((GRID_W, 2 * n1, n1), lambda i, g: (0, 0, 0)),
                  pl.BlockSpec((2 * GRID_W, 2 * GRID_W), lambda i, g: (0, 0)),
                  pl.BlockSpec((2 * HEAD_W, HEAD_W), lambda i, g: (0, 0))],
        out_specs=pl.BlockSpec((None, t, LANES), lambda i, g: (i, 0, g)),
        out_shape=jax.ShapeDtypeStruct((b, t, w), F32),
        scratch_shapes=[pltpu.VMEM((GRID_W * pitch, LANES), F32), pltpu.VMEM((GRID_W * pitch, LANES), F32)],
        compiler_params=_cparams("arbitrary", "arbitrary"),
        name="dft2",
    )(f, ftw, f2, f3)


def _outproj_kernel(fo_ref, of_ref, ob_ref, z_ref, x_ref, gate_ref, shift_ref, scale_ref,
                    og_ref, n2_ref, wo_ref, wr_ref, br_ref, x1_ref, h2_ref, rt_ref):
    o = of_ref[...].astype(F32) + ob_ref[...].astype(F32)
    z = z_ref[...].astype(F32)
    parts = [fo_ref[...].astype(BF16)]
    for h in range(DN_HEADS):
        hs = slice(h * HEAD_W, (h + 1) * HEAD_W)
        oh = o[:, hs]
        ms = jnp.mean(oh * oh, axis=-1, keepdims=True)
        on = oh * lax.rsqrt(ms + EPS) * og_ref[...]
        parts.append((on * _silu(z[:, hs])).astype(BF16))
    mix = jnp.concatenate(parts, axis=1)
    y = _dot(mix, wo_ref[...])
    x1 = x_ref[...] + gate_ref[0] * y
    x1_ref[...] = x1
    ms = jnp.mean(x1 * x1, axis=-1, keepdims=True)
    h2 = x1 * lax.rsqrt(ms + EPS) * n2_ref[...]
    h2 = h2 * (1.0 + scale_ref[0]) + shift_ref[0]
    _store_token_tiles(h2_ref, h2)
    lg = _dot(h2.astype(BF16), wr_ref[...]) + br_ref[...]
    lane = lax.broadcasted_iota(jnp.int32, lg.shape, 1)
    lane_f = lane.astype(F32)
    neg = jnp.float32(-3.0e38)
    big = jnp.float32(1.0e6)
    is_g = lane < N_GROUPS
    mg = jnp.max(jnp.where(is_g, lg, neg), axis=-1, keepdims=True)
    sg = jnp.sum(jnp.where(is_g, jnp.exp(jnp.where(is_g, lg - mg, 0.0)), 0.0), axis=-1, keepdims=True)
    g_top = jnp.min(jnp.where(is_g & (lg == mg), lane_f, big), axis=-1, keepdims=True)
    pg_top = 1.0 / sg
    eidx = lane - N_GROUPS
    eidx_f = eidx.astype(F32)
    grp_f = (eidx >> 3).astype(F32)
    in_grp = (eidx >= 0) & (eidx < N_EXPERTS) & (grp_f == g_top)
    m1 = jnp.max(jnp.where(in_grp, lg, neg), axis=-1, keepdims=True)
    i1 = jnp.min(jnp.where(in_grp & (lg == m1), eidx_f, big), axis=-1, keepdims=True)
    rest = in_grp & (eidx_f != i1)
    m2 = jnp.max(jnp.where(rest, lg, neg), axis=-1, keepdims=True)
    i2 = jnp.min(jnp.where(rest & (lg == m2), eidx_f, big), axis=-1, keepdims=True)
    e2 = jnp.exp(m2 - m1)
    w0 = pg_top / (1.0 + e2)
    w1 = pg_top * e2 / (1.0 + e2)
    rt_ref[...] = jnp.where(lane == 0, i1,
                            jnp.where(lane == 1, i2,
                                      jnp.where(lane == 2, w0, jnp.where(lane == 3, w1, 0.0))))


def _outproj(fo, of, ob, z, x2, gate1, shift2, scale2, onorm_g, norm2_g, w_out, w_group, b_group,
             w_router, b_router, tokens_per_batch):
    n, d = x2.shape
    tm = TM_OUT
    w = fo.shape[1]
    wr = jnp.pad(jnp.concatenate([w_group, w_router], axis=1), ((0, 0), (0, LANES - N_GROUPS - N_EXPERTS)))
    br =jnp.pad(jnp.concatenate([b_group, b_router]).reshape(1, -1), ((0, 0), (0, LANES - N_GROUPS - N_EXPERTS)))
    steps_per_batch = tokens_per_batch // tm
    bmap = lambda i: (i // steps_per_batch, 0, 0)
    const = lambda i: (0, 0)
    row = lambda i: (i, 0)
    return pl.pallas_call(
        _outproj_kernel,
        grid=(n // tm,),
        in_specs=[pl.BlockSpec((tm, w), row), pl.BlockSpec((tm, w), row), pl.BlockSpec((tm, w), row),
                  pl.BlockSpec((tm, w), row), pl.BlockSpec((tm, d), row),
                  pl.BlockSpec((1, 1, d), bmap), pl.BlockSpec((1, 1, d), bmap), pl.BlockSpec((1, 1, d), bmap),
                  pl.BlockSpec((1, HEAD_W), const), pl.BlockSpec((1, d), const),
                  pl.BlockSpec((d, d), const), pl.BlockSpec((d, LANES), const),
                  pl.BlockSpec((1, LANES), const)],
        out_specs=[pl.BlockSpec((tm, d), row), pl.BlockSpec((tm * ROW_TILE, LANES), row),
                   pl.BlockSpec((tm, LANES), row)],
        out_shape=[jax.ShapeDtypeStruct((n, d), F32), jax.ShapeDtypeStruct((n * ROW_TILE, LANES), F32),
                   jax.ShapeDtypeStruct((n, LANES), F32)],
        compiler_params=_cparams("arbitrary"),
        name="outproj_router",
    )(fo, of, ob, z, x2, gate1, shift2, scale2, onorm_g.reshape(1, HEAD_W), norm2_g.reshape(1, d),
      w_out.astype(BF16), wr.astype(BF16), br)


def _rank_kernel(rt_ref, rk_ref, cnt_ref, carry_ref):
    i = pl.program_id(0)

    @pl.when(i == 0)
    def _():
        carry_ref[...] = jnp.zeros_like(carry_ref)

    rt = rt_ref[...]
    tm = rt.shape[0]
    lane = lax.broadcasted_iota(jnp.int32, rt.shape, 1)
    e0 = rt[:, 0:1].astype(jnp.int32)
    e1 = rt[:, 1:2].astype(jnp.int32)
    oh0 = (lane == e0).astype(F32)
    oh1 = (lane == e1).astype(F32)
    both = oh0 + oh1
    r = lax.broadcasted_iota(jnp.int32, (tm, tm), 0)
    c = lax.broadcasted_iota(jnp.int32, (tm, tm), 1)
    lower = (r > c).astype(BF16)
    before = _dot(lower, both.astype(BF16)) + carry_ref[0:1, :]
    rank0 = jnp.sum(before * oh0, axis=-1, keepdims=True)
    rank1 = jnp.sum(before * oh1, axis=-1, keepdims=True)
    rk_ref[...] = jnp.where(lane == 0, rank0, jnp.where(lane == 1, rank1, 0.0))
    total = carry_ref[0:1, :] + jnp.sum(both, axis=0, keepdims=True)
    carry_ref[...] = jnp.broadcast_to(total, carry_ref.shape)
    cnt_ref[...] = jnp.broadcast_to(total, cnt_ref.shape)


def _rank(rt):
    n = rt.shape[0]
    tm = min(TM_RANK, n)
    return pl.pallas_call(
        _rank_kernel,
        grid=(n // tm,),
        in_specs=[pl.BlockSpec((tm, LANES), lambda i: (i, 0))],
        out_specs=[pl.BlockSpec((tm, LANES), lambda i: (i, 0)),
                   pl.BlockSpec((SUBLANES, LANES), lambda i: (0, 0))],
        out_shape=[jax.ShapeDtypeStruct((n, LANES), F32), jax.ShapeDtypeStruct((SUBLANES, LANES), F32)],
        scratch_shapes=[pltpu.VMEM((SUBLANES, LANES), F32)],
        compiler_params=_cparams("arbitrary"),
        name="moe_rank",
    )(rt)


def _dest_kernel(rt_ref, rk_ref, ps_ref, d_ref):
    rt = rt_ref[...]
    rk = rk_ref[...]
    lane = lax.broadcasted_iota(jnp.int32, rt.shape, 1)
    e0 = rt[:, 0:1].astype(jnp.int32)
    e1 = rt[:, 1:2].astype(jnp.int32)
    ps = ps_ref[0:1, :]
    d0 = jnp.sum(jnp.where(lane == e0, ps, 0.0), axis=-1, keepdims=True) + rk[:, 0:1]
    d1 = jnp.sum(jnp.where(lane == e1, ps, 0.0), axis=-1, keepdims=True) + rk[:, 1:2]
    d_ref[...] = jnp.where(lane == 0, d0, jnp.where(lane == 1, d1, 0.0)).astype(jnp.int32)


def _dest(rt, rk, pstart_row):
    n = rt.shape[0]
    tm = min(TM_DEST, n)
    return pl.pallas_call(
        _dest_kernel,
        grid=(n // tm,),
        in_specs=[pl.BlockSpec((tm, LANES), lambda i: (i, 0)),
                  pl.BlockSpec((tm, LANES), lambda i: (i, 0)),
                  pl.BlockSpec((SUBLANES, LANES), lambda i: (0, 0))],
        out_specs=pl.BlockSpec((tm, LANES), lambda i: (i, 0)),
        out_shape=jax.ShapeDtypeStruct((n, LANES), jnp.int32),
        compiler_params=_cparams("arbitrary"),
        name="moe_dest",
    )(rt, rk, pstart_row)


def _dispatch_kernel(pend_ref, na_ref, d0_ref, d1_ref, h_ref, xs_ref, zbuf, sem):
    tm = h_ref.shape[0]
    n_blocks = xs_ref.shape[0] // MOE_TILE

    @pl.when(pl.program_id(0) == 0)
    def _():
        zbuf[...] = jnp.zeros_like(zbuf)

        def zero_tile(tile):
            return pltpu.make_async_copy(zbuf, xs_ref.at[pl.ds(pl.multiple_of(tile * MOE_TILE, MOE_TILE), MOE_TILE)], sem)

        def last_tile(e):
            prev_end = jnp.where(e == 0, 0, pend_ref[jnp.maximum(e - 1, 0)])
            return pend_ref[e] > prev_end, pend_ref[e] // MOE_TILE - 1

        def start_e(e, carry):
            nonempty, tile = last_tile(e)

            @pl.when(nonempty)
            def _():
                zero_tile(tile).start()
            return carry

        def wait_e(e, carry):
            nonempty, tile = last_tile(e)

            @pl.when(nonempty)
            def _():
                zero_tile(tile).wait()
            return carry

        def start_t(tile, carry):
            zero_tile(tile).start()
            return carry

        def wait_t(tile, carry):
            zero_tile(tile).wait()
            return carry

        lax.fori_loop(0, N_EXPERTS, start_e, 0)
        lax.fori_loop(na_ref[0], n_blocks, start_t, 0)
        lax.fori_loop(0, N_EXPERTS, wait_e, 0)
        lax.fori_loop(na_ref[0], n_blocks, wait_t, 0)

    def issue(g, carry):
        for u in range(ROW_UNROLL):
            t = g * ROW_UNROLL + u
            pltpu.make_async_copy(h_ref.at[t], xs_ref.at[d0_ref[0, 0, t]], sem).start()
            pltpu.make_async_copy(h_ref.at[t], xs_ref.at[d1_ref[0, 0, t]], sem).start()
        return carry

    lax.fori_loop(0, tm // ROW_UNROLL, issue, 0)

    def drain(g, carry):
        for _ in range(2 * ROW_UNROLL):
            pltpu.make_async_copy(h_ref.at[0], xs_ref.at[0], sem).wait()
        return carry

    lax.fori_loop(0, tm // ROW_UNROLL, drain, 0)


def _dispatch(h2t, d0, d1, pend, n_active, n_slots):
    n = h2t.shape[0]
    tm = min(TM_ROW, n)
    tile = h2t.shape[1:]
    smem = lambda: pl.BlockSpec((1, 1, tm), lambda i, pe, na: (i, 0, 0), memory_space=pltpu.SMEM)
    grid_spec = pltpu.PrefetchScalarGridSpec(
        num_scalar_prefetch=2,
        grid=(n // tm,),
        in_specs=[smem(), smem(), pl.BlockSpec((tm,) + tile, lambda i, pe, na: (i, 0, 0))],
        out_specs=pl.BlockSpec(memory_space=pl.ANY),
        scratch_shapes=[pltpu.VMEM((MOE_TILE,) + tile, F32), pltpu.SemaphoreType.DMA(())],
    )
    return pl.pallas_call(
        _dispatch_kernel,
        grid_spec=grid_spec,
        out_shape=jax.ShapeDtypeStruct((n_slots,) + tile, F32),
        compiler_params=_cparams("arbitrary"),
        name="moe_dispatch",
    )(pend, n_active, d0.reshape(n // tm, 1, tm), d1.reshape(n // tm, 1, tm), h2t)


def _expert_kernel(be_ref, na_ref, xs_ref, wg_ref, wu_ref, wd_ref, ys_ref, wgb, wub, wdb):
    i = pl.program_id(0)
    prev = be_ref[jnp.maximum(i - 1, 0)]
    fresh = (i == 0) | (be_ref[i] != prev)
    active = i < na_ref[0]

    @pl.when(active & fresh)
    def _():
        wgb[...] = wg_ref[0].astype(BF16)
        wub[...] = wu_ref[0].astype(BF16)
        wdb[...] = wd_ref[0].astype(BF16)

    @pl.when(active)
    def _():
        half = xs_ref.shape[0] // ROW_TILE // 2
        xa = _load_token_tiles(xs_ref, half).astype(BF16)
        xb = _load_token_tiles(xs_ref, half, row0=half).astype(BF16)
        ga, ua = _dot(xa, wgb[...]), _dot(xa, wub[...])
        gb_, ub = _dot(xb, wgb[...]), _dot(xb, wub[...])
        _store_token_tiles(ys_ref, _dot((_silu(ga) * ua).astype(BF16), wdb[...]))
        _store_token_tiles(ys_ref, _dot((_silu(gb_) * ub).astype(BF16), wdb[...]), row0=half)

    @pl.when(jnp.logical_not(active))
    def _():
        ys_ref[...] = jnp.zeros_like(ys_ref)


def _experts(xs, blk_expert, n_active, w_gate, w_up, w_down):
    n_slots = xs.shape[0] // ROW_TILE
    n_blocks = n_slots // MOE_TILE
    d, de = w_gate.shape[1], w_gate.shape[2]
    xmap = lambda i, be, na: (jnp.minimum(i, na[0] - 1), 0)
    wmap = lambda i, be, na: (be[i], 0, 0)
    grid_spec = pltpu.PrefetchScalarGridSpec(
        num_scalar_prefetch=2,
        grid=(n_blocks,),
        in_specs=[pl.BlockSpec((MOE_TILE * ROW_TILE, LANES), xmap),
                  pl.BlockSpec((1, d, de), wmap),
                  pl.BlockSpec((1, d, de), wmap),
                  pl.BlockSpec((1, de, d), wmap)],
        out_specs=pl.BlockSpec((MOE_TILE * ROW_TILE, LANES), lambda i, be, na: (i, 0)),
        scratch_shapes=[pltpu.VMEM((d, de), BF16), pltpu.VMEM((d, de), BF16), pltpu.VMEM((de, d), BF16)],
    )
    return pl.pallas_call(
        _expert_kernel,
        grid_spec=grid_spec,
        out_shape=jax.ShapeDtypeStruct((n_slots * ROW_TILE, LANES), F32),
        compiler_params=_cparams("arbitrary"),
        name="moe_experts",
    )(blk_expert, n_active, xs, w_gate, w_up, w_down)


def _combine_kernel(d0c_ref, d1c_ref, d0n_ref, d1n_ref, ys_ref, rt_ref, x1_ref, gate_ref, fg_ref, o_ref,
                    ya, yb, sem):
    tm = x1_ref.shape[0]
    i = pl.program_id(0)
    slot = i % 2

    def gather_block(d0_ref, d1_ref, s):
        def issue(g, carry):
            for u in range(ROW_UNROLL):
                t = g * ROW_UNROLL + u
                row = pl.ds(pl.multiple_of(t * ROW_TILE, ROW_TILE), ROW_TILE)
                pltpu.make_async_copy(ys_ref.at[d0_ref[0, 0, t]], ya.at[s, row], sem.at[s]).start()
                pltpu.make_async_copy(ys_ref.at[d1_ref[0, 0, t]], yb.at[s, row], sem.at[s]).start()
            return carry

        lax.fori_loop(0, tm // ROW_UNROLL, issue, 0)

    @pl.when(i == 0)
    def _():
        gather_block(d0c_ref, d1c_ref, 0)

    @pl.when(i + 1 < pl.num_programs(0))
    def _():
        gather_block(d0n_ref, d1n_ref, 1 - slot)

    def drain(g, carry):
        for _ in range(ROW_UNROLL):
            row = pl.ds(0, ROW_TILE)
            pltpu.make_async_copy(ys_ref.at[0], ya.at[slot, row], sem.at[slot]).wait()
            pltpu.make_async_copy(ys_ref.at[0], yb.at[slot, row], sem.at[slot]).wait()
        return carry

    lax.fori_loop(0, tm // ROW_UNROLL, drain, 0)

    def finish(s):
        rt = rt_ref[...]
        moe = rt[:, 2:3] * _load_token_tiles(ya.at[s], tm) + rt[:, 3:4] * _load_token_tiles(yb.at[s], tm)
        xo = x1_ref[...] + gate_ref[0] * moe
        ms = jnp.mean(xo * xo, axis=-1, keepdims=True)
        o_ref[...] = xo * lax.rsqrt(ms + EPS) * fg_ref[...]

    for s in range(2):
        pl.when(slot == s)(functools.partial(finish, s))


def _combine(ys, d0, d1, rt, x1, gate2, final_g, tokens_per_batch):
    n, d = x1.shape
    tm = min(TM_ROW, tokens_per_batch)
    steps_per_batch = tokens_per_batch // tm
    steps = n // tm
    cur = lambda: pl.BlockSpec((1, 1, tm), lambda i: (i, 0, 0), memory_space=pltpu.SMEM)
    nxt = lambda: pl.BlockSpec((1, 1, tm), lambda i: (jnp.minimum(i + 1, steps - 1), 0, 0),
                               memory_space=pltpu.SMEM)
    d0r = d0.reshape(steps, 1, tm)
    d1r = d1.reshape(steps, 1, tm)
    return pl.pallas_call(
        _combine_kernel,
        grid=(steps,),
        in_specs=[cur(), cur(), nxt(), nxt(),
                  pl.BlockSpec(memory_space=pl.ANY),
                  pl.BlockSpec((tm, LANES), lambda i: (i, 0)),
                  pl.BlockSpec((tm, d), lambda i: (i, 0)),
                  pl.BlockSpec((1, 1, d), lambda i: (i // steps_per_batch, 0, 0)),
                  pl.BlockSpec((1, d), lambda i: (0, 0))],
        out_specs=pl.BlockSpec((tm, d), lambda i: (i, 0)),
        out_shape=jax.ShapeDtypeStruct((n, d), F32),
        scratch_shapes=[pltpu.VMEM((2, tm * ROW_TILE, LANES), F32), pltpu.VMEM((2, tm * ROW_TILE, LANES), F32),
                        pltpu.SemaphoreType.DMA((2,))],
        compiler_params=_cparams("arbitrary"),
        name="moe_combine",
    )(d0r, d1r, d0r, d1r, ys, rt, x1, gate2, final_g.reshape(1, d))


def _moe_and_final(h2, rt, x1, gate2, final_g, w_gate, w_up, w_down, tokens_per_batch):
    n = x1.shape[0]
    rk, cnt = _rank(rt)
    counts = cnt[0, :N_EXPERTS].astype(jnp.int32)
    pcounts = (counts + MOE_TILE - 1) // MOE_TILE * MOE_TILE
    pend = jnp.cumsum(pcounts)
    pstart = pend - pcounts
    n_slots = -(-(2 * n) // MOE_TILE) * MOE_TILE + N_EXPERTS * MOE_TILE
    n_blocks = n_slots // MOE_TILE
    blk_start = jnp.arange(n_blocks, dtype=jnp.int32) * MOE_TILE
    blk_expert = jnp.minimum(jnp.sum((pend[None, :] <= blk_start[:, None]).astype(jnp.int32), axis=1),
                             N_EXPERTS - 1)
    n_active = (pend[-1:] // MOE_TILE).astype(jnp.int32)
    ps_row = jnp.broadcast_to(jnp.pad(pstart.astype(F32), (0, LANES - N_EXPERTS))[None, :], (SUBLANES, LANES))
    dest = _dest(rt, rk, ps_row)
    d0 = dest[:, 0]
    d1 = dest[:, 1]
    xs = _dispatch(h2.reshape(n, ROW_TILE, LANES), d0, d1, pend.astype(jnp.int32), n_active, n_slots)
    ys = _experts(xs.reshape(n_slots * ROW_TILE, LANES), blk_expert, n_active, w_gate, w_up, w_down)
    return _combine(ys.reshape(n_slots, ROW_TILE, LANES), d0, d1, rt, x1, gate2, final_g, tokens_per_batch)


def _layer(x, c, ctx, c_ctx, w_mod, b_mod, norm1_g, w_in, conv_w, a_log, dt_bias, onorm_g, w_out,
           norm2_g, w_group, b_group, w_router, b_router, w_gate, w_up, w_down, final_g):
    b, t, d = x.shape
    tc = ctx.shape[1]
    rows = -(-(b + 1) // SUBLANES) * SUBLANES
    cc = jnp.zeros((rows, d), F32).at[:b].set(c).at[b].set(c_ctx)
    mod = _adaln(cc, w_mod, b_mod)
    mx = [mod[:b, i * d:(i + 1) * d].reshape(b, 1, d) for i in range(6)]
    mc = [jnp.broadcast_to(mod[b:b + 1, i * d:(i + 1) * d].reshape(1, 1, d), (b, 1, d)) for i in range(2)]

    _, qkv_c, _, gb_c, gt_c = _inproj(ctx.reshape(b * tc, d), mc[0], mc[1], norm1_g, w_in, a_log, dt_bias, tc)
    feat_c = _conv_features(qkv_c.reshape(b, tc, -1), conv_w, grid_mode=False)
    gb_c = gb_c.reshape(b, tc, LANES)
    zero_state = jnp.zeros((b, DN_HEADS, HEAD_W, HEAD_W), F32)
    _, s_fwd = _delta_scan(feat_c, gb_c, gt_c, zero_state, reverse=False)
    _, s_bwd = _delta_scan(feat_c, gb_c, gt_c, zero_state, reverse=True)

    x2 = x.reshape(b * t, d)
    f, qkv, z, gb, gt = _inproj(x2, mx[0], mx[1], norm1_g, w_in, a_log, dt_bias, t)
    feat = _conv_features(qkv.reshape(b, t, -1), conv_w, grid_mode=True)
    gb = gb.reshape(b, t, LANES)
    o_f, _ = _delta_scan(feat, gb, gt, s_fwd, reverse=False)
    o_b, _ = _delta_scan(feat, gb, gt, s_bwd, reverse=True)
    fo = _fourier_mix(f.reshape(b, t, -1))
    x1, h2, rt = _outproj(fo.reshape(b * t, -1), o_f.reshape(b * t, -1), o_b.reshape(b * t, -1), z, x2,
                          mx[2], mx[3], mx[4], onorm_g, norm2_g, w_out, w_group, b_group, w_router,
                          b_router, t)
    out = _moe_and_final(h2, rt, x1, mx[5], final_g, w_gate, w_up, w_down, t)
    return out.reshape(b, t, d)


def kernel(x, c, ctx, c_ctx, w_mod, b_mod, norm1_g, w_in, conv_w, a_log, dt_bias, onorm_g, w_out, norm2_g,
           w_group, b_group, w_router, b_router, w_gate, w_up, w_down, final_g):
    assert w_mod.shape[0] == 1, "single-layer trunk"
    return _layer(x, c, ctx, c_ctx, w_mod[0], b_mod[0], norm1_g[0], w_in[0], conv_w[0], a_log[0], dt_bias[0],
                  onorm_g[0], w_out[0], norm2_g[0], w_group[0], b_group[0], w_router[0], b_router[0],
                  w_gate[0], w_up[0], w_down[0], final_g)
```

```python
import functools
import math

import numpy as np
import jax
import jax.numpy as jnp
from jax import lax
from jax.experimental import pallas as pl
from jax.experimental.pallas import tpu as pltpu

F32 = jnp.float32
BF16 = jnp.bfloat16

GRID_W = 64
F_GROUPS = 4
DN_HEADS = 4
HEAD_W = 128
CHUNK = 64
N_GROUPS = 4
EXPERTS_PER_GROUP = 8
N_EXPERTS = N_GROUPS * EXPERTS_PER_GROUP
EPS = 1e-6

LANES = 128
SUBLANES = 8
VMEM_LIMIT = 56 * 1024 * 1024

TM_IN = 1024
TM_OUT = 512
TM_RANK = 1024
TM_DEST = 2048
TM_ROW = 256
MOE_TILE = 512
ROW_UNROLL = 8
DELTA_CB = 8
FFT_TB = 8
CONV_ROWS = 256
CONV_PAD = 72


def _cparams(*sem):
    return pltpu.CompilerParams(dimension_semantics=sem, vmem_limit_bytes=VMEM_LIMIT)


def _silu(v):
    return v * jax.nn.sigmoid(v)


def _dot(a, b):
    return jnp.dot(a, b, preferred_element_type=F32)


ROW_TILE = SUBLANES


def _store_token_tiles(ref, val, row0=0):
    rows = val.shape[0]
    for j in range(val.shape[1] // LANES):
        ref[pl.ds(row0 * ROW_TILE + j, rows, stride=ROW_TILE), :] = val[:, j * LANES:(j + 1) * LANES]


def _load_token_tiles(ref, rows, row0=0, width=ROW_TILE * LANES):
    return jnp.concatenate(
        [ref[pl.ds(row0 * ROW_TILE + j, rows, stride=ROW_TILE), :] for j in range(width // LANES)], axis=1)


def _adaln_kernel(c_ref, w_ref, b_ref, o_ref):
    a = _silu(c_ref[...])
    o_ref[...] = jnp.dot(a, w_ref[...], preferred_element_type=F32,
                         precision=lax.Precision.HIGHEST) + b_ref[...]


def _adaln(cc, w_mod, b_mod):
    rows, d = cc.shape
    n = w_mod.shape[1]
    tn = 1024
    return pl.pallas_call(
        _adaln_kernel,
        grid=(n // tn,),
        in_specs=[pl.BlockSpec((rows, d), lambda j: (0, 0)),
                  pl.BlockSpec((d, tn), lambda j: (0, j)),
                  pl.BlockSpec((1, tn), lambda j: (0, j))],
        out_specs=pl.BlockSpec((rows, tn), lambda j: (0, j)),
        out_shape=jax.ShapeDtypeStruct((rows, n), F32),
        compiler_params=_cparams("arbitrary"),
        name="adaln",
    )(cc, w_mod, b_mod.reshape(1, n))


def _modulated_norm(x, g, shift, scale):
    ms = jnp.mean(x * x, axis=-1, keepdims=True)
    h = x * lax.rsqrt(ms + EPS) * g
    return (h * (1.0 + scale) + shift).astype(BF16)


def _gate_features(gates, alog_ref, dtb_ref):
    tm = gates.shape[0]
    lane = lax.broadcasted_iota(jnp.int32, gates.shape, 1)
    pos = lax.broadcasted_iota(jnp.int32, gates.shape, 0) & (CHUNK - 1)
    beta = jax.nn.sigmoid(gates)
    sp_in = gates + dtb_ref[...]
    softplus = jnp.maximum(sp_in, 0.0) + jnp.log1p(jnp.exp(-jnp.abs(sp_in)))
    g = -jnp.exp(alog_ref[...]) * softplus
    g = jnp.where((lane >= 8) & (lane < 16), g, 0.0)
    pre = g
    suf = g
    s = 1
    while s < CHUNK:
        pre = pre + jnp.where(pos >= s, pltpu.roll(pre, s, 0), 0.0)
        suf = suf + jnp.where(pos < CHUNK - s, pltpu.roll(suf, tm - s, 0), 0.0)
        s *= 2
    return jnp.where(lane < 8, beta, jnp.where(lane < 12, pre, suf))


def _inproj_kernel(x_ref, shift_ref, scale_ref, g_ref, wf_ref, wqkv_ref, wz_ref, wg_ref,
                   alog_ref, dtb_ref, f_ref, qkv_ref, z_ref, gb_ref, gt_ref):
    hb = _modulated_norm(x_ref[...], g_ref[...], shift_ref[0], scale_ref[0])
    f_ref[...] = _dot(hb, wf_ref[...])
    qkv_ref[...] = _dot(hb, wqkv_ref[...])
    z_ref[...] = _dot(hb, wz_ref[...]).astype(BF16)
    gb = _gate_features(_dot(hb, wg_ref[...]), alog_ref, dtb_ref)
    gb_ref[...] = gb
    gt_ref[...] = gb.T[8:16, :]


def _split_w_in(w_in, a_log, dt_bias):
    f_w = F_GROUPS * HEAD_W
    qkv_w = 3 * DN_HEADS * HEAD_W
    z_w = DN_HEADS * HEAD_W
    wb = w_in.astype(BF16)
    wg = jnp.pad(wb[:, f_w + qkv_w + z_w:], ((0, 0), (0, LANES - 4 * DN_HEADS)))
    alog = jnp.pad(a_log.reshape(1, -1), ((0, 0), (8, LANES - 16)))
    dtb = jnp.pad(dt_bias.reshape(1, -1), ((0, 0), (8, LANES - 16)))
    return wb[:, :f_w], wb[:, f_w:f_w + qkv_w], wb[:, f_w + qkv_w:f_w + qkv_w + z_w], wg, alog, dtb


def _inproj(x2, shift, scale, norm_g, w_in, a_log, dt_bias, tokens_per_batch):
    n, d = x2.shape
    tm = min(TM_IN, tokens_per_batch)
    wf, wqkv, wz, wg, alog, dtb = _split_w_in(w_in, a_log, dt_bias)
    f_w, qkv_w, z_w = wf.shape[1], wqkv.shape[1], wz.shape[1]
    steps_per_batch = tokens_per_batch // tm
    bmap = lambda i: (i // steps_per_batch, 0, 0)
    const = lambda i: (0, 0)
    row = lambda i: (i, 0)
    return pl.pallas_call(
        _inproj_kernel,
        grid=(n // tm,),
        in_specs=[pl.BlockSpec((tm, d), row),
                  pl.BlockSpec((1, 1, d), bmap),
                  pl.BlockSpec((1, 1, d), bmap),
                  pl.BlockSpec((1, d), const),
                  pl.BlockSpec((d, f_w), const),
                  pl.BlockSpec((d, qkv_w), const),
                  pl.BlockSpec((d, z_w), const),
                  pl.BlockSpec((d, LANES), const),
                  pl.BlockSpec((1, LANES), const),
                  pl.BlockSpec((1, LANES), const)],
        out_specs=[pl.BlockSpec((tm, f_w), row),
                   pl.BlockSpec((tm, qkv_w), row),
                   pl.BlockSpec((tm, z_w), row),
                   pl.BlockSpec((tm, LANES), row),
                   pl.BlockSpec((SUBLANES, tm), lambda i: (0, i))],
        out_shape=[jax.ShapeDtypeStruct((n, f_w), F32),
                   jax.ShapeDtypeStruct((n, qkv_w), F32),
                   jax.ShapeDtypeStruct((n, z_w), BF16),
                   jax.ShapeDtypeStruct((n, LANES), F32),
                   jax.ShapeDtypeStruct((SUBLANES, n), F32)],
        compiler_params=_cparams("arbitrary"),
        name="inproj",
    )(x2, shift, scale, norm_g.reshape(1, d), wf, wqkv, wz, wg, alog, dtb)


def _conv_rows(src, row0, lanes, w, rows, grid_mode):
    col = lax.broadcasted_iota(jnp.int32, (rows, LANES), 0) & (GRID_W - 1)
    acc = None
    for dc in (0, -1, 1):
        part = None
        for dr in ((-1, 0, 1) if grid_mode else (0,)):
            off = row0 + GRID_W * dr + dc
            tap = 3 * (dr + 1) + (dc + 1)
            term = src[off:off + rows, lanes] * w[tap:tap + 1, :]
            part = term if part is None else part + term
        if grid_mode and dc == -1:
            part = jnp.where(col != 0, part, 0.0)
        if grid_mode and dc == 1:
            part = jnp.where(col != GRID_W - 1, part, 0.0)
        acc = part if acc is None else acc + part
    return _silu(acc)


def _conv_kernel(x_ref, w_ref, o_ref, pad_ref, *, grid_mode):
    t = x_ref.shape[1]
    j = pl.program_id(1)
    zeros = jnp.zeros((CONV_PAD, x_ref.shape[2]), F32)
    pad_ref[0:CONV_PAD, :] = zeros
    pad_ref[CONV_PAD + t:CONV_PAD + t + CONV_PAD, :] = zeros
    pad_ref[CONV_PAD:CONV_PAD + t, :] = x_ref[0]
    rows = min(CONV_ROWS, t)
    tiles = x_ref.shape[2] // LANES
    for lt in range(tiles):
        lanes = slice(lt * LANES, (lt + 1) * LANES)
        head = j * tiles + lt
        w = w_ref[:, lanes]
        for r0 in range(0, t, rows):
            y = _conv_rows(pad_ref, CONV_PAD + r0, lanes, w, rows, grid_mode)
            inv = lax.rsqrt(jnp.sum(y * y, axis=-1, keepdims=True) + EPS)
            fac = jnp.where(head < DN_HEADS, inv * (HEAD_W ** -0.5), jnp.where(head < 2 * DN_HEADS, inv, 1.0))
            o_ref[0, r0:r0 + rows, lanes] = y * fac


def _conv_features(qkv, conv_w, grid_mode):
    b, t, ch = qkv.shape
    w9 = conv_w.reshape(9, ch)
    width = LANES if t * ch * 4 > (4 << 20) else ch
    return pl.pallas_call(
        functools.partial(_conv_kernel, grid_mode=grid_mode),
        grid=(b, ch // width),
        in_specs=[pl.BlockSpec((1, t, width), lambda i, j: (i, 0, j)),
                  pl.BlockSpec((9, width), lambda i, j: (0, j))],
        out_specs=pl.BlockSpec((1, t, width), lambda i, j: (i, 0, j)),
        out_shape=jax.ShapeDtypeStruct((b, t, ch), F32),
        scratch_shapes=[pltpu.VMEM((t + 2 * CONV_PAD, width), F32)],
        compiler_params=_cparams("arbitrary", "arbitrary"),
        name="conv_grid" if grid_mode else "conv_seq",
    )(qkv, w9)


def _delta_kernel(q_ref, k_ref, v_ref, gb_ref, grow_ref, s0_ref, o_ref, sfin_ref, s_scr, *,
                  reverse, cb):
    j = pl.program_id(1)
    nh = DN_HEADS
    rr = nh * CHUNK

    @pl.when(j == 0)
    def _():
        s_scr[...] = s0_ref[0]

    lane_b = nh if reverse else 0
    lane_g = 8 + (nh if reverse else 0)
    ri = lax.broadcasted_iota(jnp.int32, (rr, rr), 0)
    ci = lax.broadcasted_iota(jnp.int32, (rr, rr), 1)
    same = (ri >> 6) == (ci >> 6)
    if reverse:
        tri = same & (ri <= ci)
        strict = same & (ri < ci)
    else:
        tri = same & (ri >= ci)
        strict = same & (ri > ci)
    eye = (ri == ci).astype(F32)

    order = list(range(cb - 1, -1, -1) if reverse else range(cb))

    prep = []
    for c in order:
        rows = slice(c * CHUNK, (c + 1) * CHUNK)
        q = q_ref[0, rows, :]
        k = k_ref[0, rows, :]
        v = v_ref[0, rows, :]
        gb = gb_ref[0, rows, :]
        grow = grow_ref[0, c:c + 1, :]
        kb_l, qc_l, kc_l, vb_l, kbe_l, qg_l, kd_l, gcb_l, egl_l = [], [], [], [], [], [], [], [], []
        for h in range(nh):
            hs = slice(h * HEAD_W, (h + 1) * HEAD_W)
            beta = jnp.broadcast_to(gb[:, lane_b + h:lane_b + h + 1], (CHUNK, HEAD_W))
            gc = jnp.broadcast_to(gb[:, lane_g + h:lane_g + h + 1], (CHUNK, HEAD_W))
            glast = gc[0:1, :] if reverse else gc[CHUNK - 1:CHUNK, :]
            eg = jnp.exp(gc)
            kh = k[:, hs]
            kbh = kh * beta
            kb_l.append(kbh)
            kc_l.append(kh)
            qc_l.append(q[:, hs])
            vb_l.append(v[:, hs] * beta)
            kbe_l.append(kbh * eg)
            qg_l.append(q[:, hs] * eg)
            kd_l.append(kh * jnp.exp(glast - gc))
            gcb_l.append(gc)
            egl_l.append(jnp.exp(glast))
        kc = jnp.concatenate(kc_l, axis=0).astype(BF16)
        lhs = jnp.concatenate(kb_l + qc_l, axis=0).astype(BF16)
        a = lax.dot_general(lhs, kc, (((1,), (1,)), ((), ())), preferred_element_type=F32)
        gcb = jnp.concatenate(gcb_l, axis=0)
        gcol = jnp.concatenate([gcb] * (rr // HEAD_W), axis=1)
        diff = gcol - grow
        dec = jnp.where(tri, jnp.exp(jnp.where(tri, diff, 0.0)), 0.0)
        nm = -jnp.where(strict, a[:rr] * dec, 0.0)
        intra = (a[rr:] * dec).astype(BF16)
        rhs = jnp.concatenate([jnp.concatenate(vb_l, axis=0), jnp.concatenate(kbe_l, axis=0)],
                              axis=1).astype(BF16)
        kdt_l = [kd.T.astype(BF16) for kd in kd_l]
        prep.append((rows, nm, intra, rhs, jnp.concatenate(qg_l, axis=0), kdt_l, egl_l))

    p_l = [eye + pr[1] for pr in prep]
    x_l = [pr[1].astype(BF16) for pr in prep]
    x_l = [_dot(xp, xp).astype(BF16) for xp in x_l]
    step = 2
    while step < CHUNK // 2:
        res = [_dot(jnp.concatenate([xp, p.astype(BF16)], axis=0), xp) for p, xp in zip(p_l, x_l)]
        p_l = [p + r[rr:] for p, r in zip(p_l, res)]
        x_l = [r[:rr].astype(BF16) for r in res]
        step *= 2
    p_l = [p + _dot(p.astype(BF16), xp) for p, xp in zip(p_l, x_l)]

    loc = []
    for (rows, _, intra, rhs, qg, kdt_l, egl_l), p in zip(prep, p_l):
        uwb = _dot(p.astype(BF16), rhs).astype(BF16)
        iw = _dot(intra, uwb)
        qe = (qg - iw[:, HEAD_W:]).astype(BF16)
        gq_l = [_dot(kdt_l[h], uwb[h * CHUNK:(h + 1) * CHUNK, :]) for h in range(nh)]
        loc.append((rows, iw[:, :HEAD_W], qe, gq_l, egl_l))

    for rows, o_loc, qe, gq_l, egl_l in loc:
        for h in range(nh):
            hr = slice(h * CHUNK, (h + 1) * CHUNK)
            s_h = s_scr[h]
            lhs = jnp.concatenate([gq_l[h][:, HEAD_W:].astype(BF16), qe[hr, :]], axis=0)
            rs = _dot(lhs, s_h.astype(BF16))
            o_ref[0, rows, h * HEAD_W:(h + 1) * HEAD_W] = (o_loc[hr, :] + rs[HEAD_W:]).astype(BF16)
            s_scr[h] = egl_l[h] * s_h - rs[:HEAD_W] + gq_l[h][:, :HEAD_W]

    @pl.when(j == pl.num_programs(1) - 1)
    def _():
        sfin_ref[0] = s_scr[...]


def _delta_scan(qkv, gb, gt, s0, reverse):
    b, t, _ = qkv.shape
    n_chunks = t // CHUNK
    cb = next(c for c in (DELTA_CB, 8, 4, 2, 1) if n_chunks % c == 0)
    nb = n_chunks // cb
    row0 = DN_HEADS if reverse else 0
    grow = gt[row0:row0 + DN_HEADS].reshape(DN_HEADS, b, n_chunks, CHUNK)
    grow = grow.transpose(1, 2, 0, 3).reshape(b * nb, cb, DN_HEADS * CHUNK)
    w = DN_HEADS * HEAD_W
    blk = (lambda j: nb - 1 - j) if reverse else (lambda j: j)
    return pl.pallas_call(
        functools.partial(_delta_kernel, reverse=reverse, cb=cb),
        grid=(b, nb),
        in_specs=[pl.BlockSpec((1, cb * CHUNK, w), lambda i, j: (i, blk(j), 0)),
                  pl.BlockSpec((1, cb * CHUNK, w), lambda i, j: (i, blk(j), 1)),
                  pl.BlockSpec((1, cb * CHUNK, w), lambda i, j: (i, blk(j), 2)),
                  pl.BlockSpec((1, cb * CHUNK, LANES), lambda i, j: (i, blk(j), 0)),
                  pl.BlockSpec((1, cb, DN_HEADS * CHUNK), lambda i, j: (i * nb + blk(j), 0, 0)),
                  pl.BlockSpec((1, DN_HEADS, HEAD_W, HEAD_W), lambda i, j: (i, 0, 0, 0))],
        out_specs=[pl.BlockSpec((1, cb * CHUNK, w), lambda i, j: (i, blk(j), 0)),
                   pl.BlockSpec((1, DN_HEADS, HEAD_W, HEAD_W), lambda i, j: (i, 0, 0, 0))],
        out_shape=[jax.ShapeDtypeStruct((b, t, w), BF16),
                   jax.ShapeDtypeStruct((b, DN_HEADS, HEAD_W, HEAD_W), F32)],
        scratch_shapes=[pltpu.VMEM((DN_HEADS, HEAD_W, HEAD_W), F32)],
        compiler_params=_cparams("arbitrary", "arbitrary"),
        name="delta_bwd" if reverse else "delta_fwd",
    )(qkv, qkv, qkv, gb, grow, s0)


def _dft_tables(t):
    n1 = t // GRID_W
    t1 = np.arange(n1)
    t2 = np.arange(GRID_W)
    ang = 2.0 * np.pi * (np.outer(t1, t1)[None] / n1 + (t2[:, None, None] * t1[None, :, None]) / t)
    ftw = np.concatenate([np.cos(ang), -np.sin(ang)], axis=1)
    a2 = 2.0 * np.pi * np.outer(t2, t2) / GRID_W
    c2, s2 = np.cos(a2), np.sin(a2)
    f2 = np.block([[c2, s2], [-s2, c2]])
    ch = np.arange(HEAD_W)
    a3 = 2.0 * np.pi * np.outer(ch, ch) / HEAD_W
    f3 = np.concatenate([np.cos(a3), np.sin(a3)], axis=0) / math.sqrt(t * HEAD_W)
    return (jnp.asarray(ftw, BF16), jnp.asarray(f2, BF16), jnp.asarray(f3, BF16))


def _dft_kernel(x_ref, ftw_ref, f2_ref, f3_ref, o_ref, are_ref, aim_ref):
    t = x_ref.shape[0]
    n1 = t // GRID_W
    pitch = are_ref.shape[0] // GRID_W
    for c in range(GRID_W):
        xc = x_ref[pl.ds(c, n1, stride=GRID_W), :].astype(BF16)
        a = _dot(ftw_ref[c], xc)
        are_ref[c * pitch:c * pitch + n1, :] = a[:n1]
        aim_ref[c * pitch:c * pitch + n1, :] = a[n1:]
    tb = min(FFT_TB, n1)
    for k0 in range(0, n1, tb):
        r = jnp.concatenate(
            [jnp.concatenate([are_ref[pl.ds(k0 + kk, GRID_W, stride=pitch), :],
                              aim_ref[pl.ds(k0 + kk, GRID_W, stride=pitch), :]], axis=0).astype(BF16)
             for kk in range(tb)], axis=1)
        g = _dot(f2_ref[...], r)
        gc = jnp.concatenate(
            [jnp.concatenate([g[:GRID_W, kk * LANES:(kk + 1) * LANES], g[GRID_W:, kk * LANES:(kk + 1) * LANES]],
                             axis=1) for kk in range(tb)], axis=0).astype(BF16)
        y = _dot(gc, f3_ref[...])
        for kk in range(tb):
            o_ref[pl.ds(k0 + kk, GRID_W, stride=n1), :] = y[kk * GRID_W:(kk + 1) * GRID_W]


def _fourier_mix(f):
    b, t, w = f.shape
    n1 = t // GRID_W
    ftw, f2, f3 = _dft_tables(t)
    pitch = n1 + SUBLANES
    return pl.pallas_call(
        _dft_kernel,
        grid=(b, w // LANES),
        in_specs=[pl.BlockSpec((None, t, LANES), lambda i, g: (i, 0, g)),
                  pl.BlockSpec((GRID_W, 2 * n1, n1), lambda i, g: (0, 0, 0)),
                  pl.BlockSpec((2 * GRID_W, 2 * GRID_W), lambda i, g: (0, 0)),
                  pl.BlockSpec((2 * HEAD_W, HEAD_W), lambda i, g: (0, 0))],
        out_specs=pl.BlockSpec((None, t, LANES), lambda i, g: (i, 0, g)),
        out_shape=jax.ShapeDtypeStruct((b, t, w), F32),
        scratch_shapes=[pltpu.VMEM((GRID_W * pitch, LANES), F32), pltpu.VMEM((GRID_W * pitch, LANES), F32)],
        compiler_params=_cparams("arbitrary", "arbitrary"),
        name="dft2",
    )(f, ftw, f2, f3)


def _outproj_kernel(fo_ref, of_ref, ob_ref, z_ref, x_ref, gate_ref, shift_ref, scale_ref,
                    og_ref, n2_ref, wo_ref, wr_ref, br_ref, x1_ref, h2_ref, rt_ref):
    o = of_ref[...].astype(F32) + ob_ref[...].astype(F32)
    z = z_ref[...].astype(F32)
    parts = [fo_ref[...].astype(BF16)]
    for h in range(DN_HEADS):
        hs = slice(h * HEAD_W, (h + 1) * HEAD_W)
        oh = o[:, hs]
        ms = jnp.mean(oh * oh, axis=-1, keepdims=True)
        on = oh * lax.rsqrt(ms + EPS) * og_ref[...]
        parts.append((on * _silu(z[:, hs])).astype(BF16))
    mix = jnp.concatenate(parts, axis=1)
    y = _dot(mix, wo_ref[...])
    x1 = x_ref[...] + gate_ref[0] * y
    x1_ref[...] = x1
    ms = jnp.mean(x1 * x1, axis=-1, keepdims=True)
    h2 = x1 * lax.rsqrt(ms + EPS) * n2_ref[...]
    h2 = h2 * (1.0 + scale_ref[0]) + shift_ref[0]
    _store_token_tiles(h2_ref, h2)
    lg = _dot(h2.astype(BF16), wr_ref[...]) + br_ref[...]
    lane = lax.broadcasted_iota(jnp.int32, lg.shape, 1)
    lane_f = lane.astype(F32)
    neg = jnp.float32(-3.0e38)
    big = jnp.float32(1.0e6)
    is_g = lane < N_GROUPS
    mg = jnp.max(jnp.where(is_g, lg, neg), axis=-1, keepdims=True)
    sg = jnp.sum(jnp.where(is_g, jnp.exp(jnp.where(is_g, lg - mg, 0.0)), 0.0), axis=-1, keepdims=True)
    g_top = jnp.min(jnp.where(is_g & (lg == mg), lane_f, big), axis=-1, keepdims=True)
    pg_top = 1.0 / sg
    eidx = lane - N_GROUPS
    eidx_f = eidx.astype(F32)
    grp_f = (eidx >> 3).astype(F32)
    in_grp = (eidx >= 0) & (eidx < N_EXPERTS) & (grp_f == g_top)
    m1 = jnp.max(jnp.where(in_grp, lg, neg), axis=-1, keepdims=True)
    i1 = jnp.min(jnp.where(in_grp & (lg == m1), eidx_f, big), axis=-1, keepdims=True)
    rest = in_grp & (eidx_f != i1)
    m2 = jnp.max(jnp.where(rest, lg, neg), axis=-1, keepdims=True)
    i2 = jnp.min(jnp.where(rest & (lg == m2), eidx_f, big), axis=-1, keepdims=True)
    e2 = jnp.exp(m2 - m1)
    w0 = pg_top / (1.0 + e2)
    w1 = pg_top * e2 / (1.0 + e2)
    rt_ref[...] = jnp.where(lane == 0, i1,
                            jnp.where(lane == 1, i2,
                                      jnp.where(lane == 2, w0, jnp.where(lane == 3, w1, 0.0))))


def _outproj(fo, of, ob, z, x2, gate1, shift2, scale2, onorm_g, norm2_g, w_out, w_group, b_group,
             w_router, b_router, tokens_per_batch):
    n, d = x2.shape
    tm = TM_OUT
    w = fo.shape[1]
    wr = jnp.pad(jnp.concatenate([w_group, w_router], axis=1), ((0, 0), (0, LANES - N_GROUPS - N_EXPERTS)))
    br =jnp.pad(jnp.concatenate([b_group, b_router]).reshape(1, -1), ((0, 0), (0, LANES - N_GROUPS - N_EXPERTS)))
    steps_per_batch = tokens_per_batch // tm
    bmap = lambda i: (i // steps_per_batch, 0, 0)
    const = lambda i: (0, 0)
    row = lambda i: (i, 0)
    return pl.pallas_call(
        _outproj_kernel,
        grid=(n // tm,),
        in_specs=[pl.BlockSpec((tm, w), row), pl.BlockSpec((tm, w), row), pl.BlockSpec((tm, w), row),
                  pl.BlockSpec((tm, w), row), pl.BlockSpec((tm, d), row),
                  pl.BlockSpec((1, 1, d), bmap), pl.BlockSpec((1, 1, d), bmap), pl.BlockSpec((1, 1, d), bmap),
                  pl.BlockSpec((1, HEAD_W), const), pl.BlockSpec((1, d), const),
                  pl.BlockSpec((d, d), const), pl.BlockSpec((d, LANES), const),
                  pl.BlockSpec((1, LANES), const)],
        out_specs=[pl.BlockSpec((tm, d), row), pl.BlockSpec((tm * ROW_TILE, LANES), row),
                   pl.BlockSpec((tm, LANES), row)],
        out_shape=[jax.ShapeDtypeStruct((n, d), F32), jax.ShapeDtypeStruct((n * ROW_TILE, LANES), F32),
                   jax.ShapeDtypeStruct((n, LANES), F32)],
        compiler_params=_cparams("arbitrary"),
        name="outproj_router",
    )(fo, of, ob, z, x2, gate1, shift2, scale2, onorm_g.reshape(1, HEAD_W), norm2_g.reshape(1, d),
      w_out.astype(BF16), wr.astype(BF16), br)


def _rank_kernel(rt_ref, rk_ref, cnt_ref, carry_ref):
    i = pl.program_id(0)

    @pl.when(i == 0)
    def _():
        carry_ref[...] = jnp.zeros_like(carry_ref)

    rt = rt_ref[...]
    tm = rt.shape[0]
    lane = lax.broadcasted_iota(jnp.int32, rt.shape, 1)
    e0 = rt[:, 0:1].astype(jnp.int32)
    e1 = rt[:, 1:2].astype(jnp.int32)
    oh0 = (lane == e0).astype(F32)
    oh1 = (lane == e1).astype(F32)
    both = oh0 + oh1
    r = lax.broadcasted_iota(jnp.int32, (tm, tm), 0)
    c = lax.broadcasted_iota(jnp.int32, (tm, tm), 1)
    lower = (r > c).astype(BF16)
    before = _dot(lower, both.astype(BF16)) + carry_ref[0:1, :]
    rank0 = jnp.sum(before * oh0, axis=-1, keepdims=True)
    rank1 = jnp.sum(before * oh1, axis=-1, keepdims=True)
    rk_ref[...] = jnp.where(lane == 0, rank0, jnp.where(lane == 1, rank1, 0.0))
    total = carry_ref[0:1, :] + jnp.sum(both, axis=0, keepdims=True)
    carry_ref[...] = jnp.broadcast_to(total, carry_ref.shape)
    cnt_ref[...] = jnp.broadcast_to(total, cnt_ref.shape)


def _rank(rt):
    n = rt.shape[0]
    tm = min(TM_RANK, n)
    return pl.pallas_call(
        _rank_kernel,
        grid=(n // tm,),
        in_specs=[pl.BlockSpec((tm, LANES), lambda i: (i, 0))],
        out_specs=[pl.BlockSpec((tm, LANES), lambda i: (i, 0)),
                   pl.BlockSpec((SUBLANES, LANES), lambda i: (0, 0))],
        out_shape=[jax.ShapeDtypeStruct((n, LANES), F32), jax.ShapeDtypeStruct((SUBLANES, LANES), F32)],
        scratch_shapes=[pltpu.VMEM((SUBLANES, LANES), F32)],
        compiler_params=_cparams("arbitrary"),
        name="moe_rank",
    )(rt)


def _dest_kernel(rt_ref, rk_ref, ps_ref, d_ref):
    rt = rt_ref[...]
    rk = rk_ref[...]
    lane = lax.broadcasted_iota(jnp.int32, rt.shape, 1)
    e0 = rt[:, 0:1].astype(jnp.int32)
    e1 = rt[:, 1:2].astype(jnp.int32)
    ps = ps_ref[0:1, :]
    d0 = jnp.sum(jnp.where(lane == e0, ps, 0.0), axis=-1, keepdims=True) + rk[:, 0:1]
    d1 = jnp.sum(jnp.where(lane == e1, ps, 0.0), axis=-1, keepdims=True) + rk[:, 1:2]
    d_ref[...] = jnp.where(lane == 0, d0, jnp.where(lane == 1, d1, 0.0)).astype(jnp.int32)


def _dest(rt, rk, pstart_row):
    n = rt.shape[0]
    tm = min(TM_DEST, n)
    return pl.pallas_call(
        _dest_kernel,
        grid=(n // tm,),
        in_specs=[pl.BlockSpec((tm, LANES), lambda i: (i, 0)),
                  pl.BlockSpec((tm, LANES), lambda i: (i, 0)),
                  pl.BlockSpec((SUBLANES, LANES), lambda i: (0, 0))],
        out_specs=pl.BlockSpec((tm, LANES), lambda i: (i, 0)),
        out_shape=jax.ShapeDtypeStruct((n, LANES), jnp.int32),
        compiler_params=_cparams("arbitrary"),
        name="moe_dest",
    )(rt, rk, pstart_row)


def _dispatch_kernel(pend_ref, na_ref, d0_ref, d1_ref, h_ref, xs_ref, zbuf, sem):
    tm = h_ref.shape[0]
    n_blocks = xs_ref.shape[0] // MOE_TILE

    @pl.when(pl.program_id(0) == 0)
    def _():
        zbuf[...] = jnp.zeros_like(zbuf)

        def zero_tile(tile):
            return pltpu.make_async_copy(zbuf, xs_ref.at[pl.ds(pl.multiple_of(tile * MOE_TILE, MOE_TILE), MOE_TILE)], sem)

        def last_tile(e):
            prev_end = jnp.where(e == 0, 0, pend_ref[jnp.maximum(e - 1, 0)])
            return pend_ref[e] > prev_end, pend_ref[e] // MOE_TILE - 1

        def start_e(e, carry):
            nonempty, tile = last_tile(e)

            @pl.when(nonempty)
            def _():
                zero_tile(tile).start()
            return carry

        def wait_e(e, carry):
            nonempty, tile = last_tile(e)

            @pl.when(nonempty)
            def _():
                zero_tile(tile).wait()
            return carry

        def start_t(tile, carry):
            zero_tile(tile).start()
            return carry

        def wait_t(tile, carry):
            zero_tile(tile).wait()
            return carry

        lax.fori_loop(0, N_EXPERTS, start_e, 0)
        lax.fori_loop(na_ref[0], n_blocks, start_t, 0)
        lax.fori_loop(0, N_EXPERTS, wait_e, 0)
        lax.fori_loop(na_ref[0], n_blocks, wait_t, 0)

    def issue(g, carry):
        for u in range(ROW_UNROLL):
            t = g * ROW_UNROLL + u
            pltpu.make_async_copy(h_ref.at[t], xs_ref.at[d0_ref[0, 0, t]], sem).start(priority=0)
            pltpu.make_async_copy(h_ref.at[t], xs_ref.at[d1_ref[0, 0, t]], sem).start(priority=1)
        return carry

    lax.fori_loop(0, tm // ROW_UNROLL, issue, 0)

    def drain(g, carry):
        for _ in range(2 * ROW_UNROLL):
            pltpu.make_async_copy(h_ref.at[0], xs_ref.at[0], sem).wait()
        return carry

    lax.fori_loop(0, tm // ROW_UNROLL, drain, 0)


def _dispatch(h2t, d0, d1, pend, n_active, n_slots):
    n = h2t.shape[0]
    tm = min(TM_ROW, n)
    tile = h2t.shape[1:]
    smem = lambda: pl.BlockSpec((1, 1, tm), lambda i, pe, na: (i, 0, 0), memory_space=pltpu.SMEM)
    grid_spec = pltpu.PrefetchScalarGridSpec(
        num_scalar_prefetch=2,
        grid=(n // tm,),
        in_specs=[smem(), smem(), pl.BlockSpec((tm,) + tile, lambda i, pe, na: (i, 0, 0))],
        out_specs=pl.BlockSpec(memory_space=pl.ANY),
        scratch_shapes=[pltpu.VMEM((MOE_TILE,) + tile, F32), pltpu.SemaphoreType.DMA(())],
    )
    return pl.pallas_call(
        _dispatch_kernel,
        grid_spec=grid_spec,
        out_shape=jax.ShapeDtypeStruct((n_slots,) + tile, F32),
        compiler_params=_cparams("arbitrary"),
        name="moe_dispatch",
    )(pend, n_active, d0.reshape(n // tm, 1, tm), d1.reshape(n // tm, 1, tm), h2t)


def _expert_kernel(be_ref, na_ref, xs_ref, wg_ref, wu_ref, wd_ref, ys_ref, wgb, wub, wdb):
    i = pl.program_id(0)
    prev = be_ref[jnp.maximum(i - 1, 0)]
    fresh = (i == 0) | (be_ref[i] != prev)
    active = i < na_ref[0]

    @pl.when(active & fresh)
    def _():
        wgb[...] = wg_ref[0].astype(BF16)
        wub[...] = wu_ref[0].astype(BF16)
        wdb[...] = wd_ref[0].astype(BF16)

    @pl.when(active)
    def _():
        half = xs_ref.shape[0] // ROW_TILE // 2
        xa = _load_token_tiles(xs_ref, half).astype(BF16)
        xb = _load_token_tiles(xs_ref, half, row0=half).astype(BF16)
        ga, ua = _dot(xa, wgb[...]), _dot(xa, wub[...])
        gb_, ub = _dot(xb, wgb[...]), _dot(xb, wub[...])
        _store_token_tiles(ys_ref, _dot((_silu(ga) * ua).astype(BF16), wdb[...]))
        _store_token_tiles(ys_ref, _dot((_silu(gb_) * ub).astype(BF16), wdb[...]), row0=half)

    @pl.when(jnp.logical_not(active))
    def _():
        ys_ref[...] = jnp.zeros_like(ys_ref)


def _experts(xs, blk_expert, n_active, w_gate, w_up, w_down):
    n_slots = xs.shape[0] // ROW_TILE
    n_blocks = n_slots // MOE_TILE
    d, de = w_gate.shape[1], w_gate.shape[2]
    xmap = lambda i, be, na: (jnp.minimum(i, na[0] - 1), 0)
    wmap = lambda i, be, na: (be[i], 0, 0)
    grid_spec = pltpu.PrefetchScalarGridSpec(
        num_scalar_prefetch=2,
        grid=(n_blocks,),
        in_specs=[pl.BlockSpec((MOE_TILE * ROW_TILE, LANES), xmap),
                  pl.BlockSpec((1, d, de), wmap),
                  pl.BlockSpec((1, d, de), wmap),
                  pl.BlockSpec((1, de, d), wmap)],
        out_specs=pl.BlockSpec((MOE_TILE * ROW_TILE, LANES), lambda i, be, na: (i, 0)),
        scratch_shapes=[pltpu.VMEM((d, de), BF16), pltpu.VMEM((d, de), BF16), pltpu.VMEM((de, d), BF16)],
    )
    return pl.pallas_call(
        _expert_kernel,
        grid_spec=grid_spec,
        out_shape=jax.ShapeDtypeStruct((n_slots * ROW_TILE, LANES), F32),
        compiler_params=_cparams("arbitrary"),
        name="moe_experts",
    )(blk_expert, n_active, xs, w_gate, w_up, w_down)


def _combine_kernel(d0c_ref, d1c_ref, d0n_ref, d1n_ref, ys_ref, rt_ref, x1_ref, gate_ref, fg_ref, o_ref,
                    ya, yb, sem):
    tm = x1_ref.shape[0]
    i = pl.program_id(0)
    slot = i % 2

    def gather_block(d0_ref, d1_ref, s):
        def issue(g, carry):
            for u in range(ROW_UNROLL):
                t = g * ROW_UNROLL + u
                row = pl.ds(pl.multiple_of(t * ROW_TILE, ROW_TILE), ROW_TILE)
                pltpu.make_async_copy(ys_ref.at[d0_ref[0, 0, t]], ya.at[s, row], sem.at[s]).start(priority=0)
                pltpu.make_async_copy(ys_ref.at[d1_ref[0, 0, t]], yb.at[s, row], sem.at[s]).start(priority=1)
            return carry

        lax.fori_loop(0, tm // ROW_UNROLL, issue, 0)

    @pl.when(i == 0)
    def _():
        gather_block(d0c_ref, d1c_ref, 0)

    @pl.when(i + 1 < pl.num_programs(0))
    def _():
        gather_block(d0n_ref, d1n_ref, 1 - slot)

    def drain(g, carry):
        for _ in range(ROW_UNROLL):
            row = pl.ds(0, ROW_TILE)
            pltpu.make_async_copy(ys_ref.at[0], ya.at[slot, row], sem.at[slot]).wait()
            pltpu.make_async_copy(ys_ref.at[0], yb.at[slot, row], sem.at[slot]).wait()
        return carry

    lax.fori_loop(0, tm // ROW_UNROLL, drain, 0)

    def finish(s):
        rt = rt_ref[...]
        moe = rt[:, 2:3] * _load_token_tiles(ya.at[s], tm) + rt[:, 3:4] * _load_token_tiles(yb.at[s], tm)
        xo = x1_ref[...] + gate_ref[0] * moe
        ms = jnp.mean(xo * xo, axis=-1, keepdims=True)
        o_ref[...] = xo * lax.rsqrt(ms + EPS) * fg_ref[...]

    for s in range(2):
        pl.when(slot == s)(functools.partial(finish, s))


def _combine(ys, d0, d1, rt, x1, gate2, final_g, tokens_per_batch):
    n, d = x1.shape
    tm = min(TM_ROW, tokens_per_batch)
    steps_per_batch = tokens_per_batch // tm
    steps = n // tm
    cur = lambda: pl.BlockSpec((1, 1, tm), lambda i: (i, 0, 0), memory_space=pltpu.SMEM)
    nxt = lambda: pl.BlockSpec((1, 1, tm), lambda i: (jnp.minimum(i + 1, steps - 1), 0, 0),
                               memory_space=pltpu.SMEM)
    d0r = d0.reshape(steps, 1, tm)
    d1r = d1.reshape(steps, 1, tm)
    return pl.pallas_call(
        _combine_kernel,
        grid=(steps,),
        in_specs=[cur(), cur(), nxt(), nxt(),
                  pl.BlockSpec(memory_space=pl.ANY),
                  pl.BlockSpec((tm, LANES), lambda i: (i, 0)),
                  pl.BlockSpec((tm, d), lambda i: (i, 0)),
                  pl.BlockSpec((1, 1, d), lambda i: (i // steps_per_batch, 0, 0)),
                  pl.BlockSpec((1, d), lambda i: (0, 0))],
        out_specs=pl.BlockSpec((tm, d), lambda i: (i, 0)),
        out_shape=jax.ShapeDtypeStruct((n, d), F32),
        scratch_shapes=[pltpu.VMEM((2, tm * ROW_TILE, LANES), F32), pltpu.VMEM((2, tm * ROW_TILE, LANES), F32),
                        pltpu.SemaphoreType.DMA((2,))],
        compiler_params=_cparams("arbitrary"),
        name="moe_combine",
    )(d0r, d1r, d0r, d1r, ys, rt, x1, gate2, final_g.reshape(1, d))


def _moe_and_final(h2, rt, x1, gate2, final_g, w_gate, w_up, w_down, tokens_per_batch):
    n = x1.shape[0]
    rk, cnt = _rank(rt)
    counts = cnt[0, :N_EXPERTS].astype(jnp.int32)
    pcounts = (counts + MOE_TILE - 1) // MOE_TILE * MOE_TILE
    pend = jnp.cumsum(pcounts)
    pstart = pend - pcounts
    n_slots = -(-(2 * n) // MOE_TILE) * MOE_TILE + N_EXPERTS * MOE_TILE
    n_blocks = n_slots // MOE_TILE
    blk_start = jnp.arange(n_blocks, dtype=jnp.int32) * MOE_TILE
    blk_expert = jnp.minimum(jnp.sum((pend[None, :] <= blk_start[:, None]).astype(jnp.int32), axis=1),
                             N_EXPERTS - 1)
    n_active = (pend[-1:] // MOE_TILE).astype(jnp.int32)
    ps_row = jnp.broadcast_to(jnp.pad(pstart.astype(F32), (0, LANES - N_EXPERTS))[None, :], (SUBLANES, LANES))
    dest = _dest(rt, rk, ps_row)
    d0 = dest[:, 0]
    d1 = dest[:, 1]
    xs = _dispatch(h2.reshape(n, ROW_TILE, LANES), d0, d1, pend.astype(jnp.int32), n_active, n_slots)
    ys = _experts(xs.reshape(n_slots * ROW_TILE, LANES), blk_expert, n_active, w_gate, w_up, w_down)
    return _combine(ys.reshape(n_slots, ROW_TILE, LANES), d0, d1, rt, x1, gate2, final_g, tokens_per_batch)


def _layer(x, c, ctx, c_ctx, w_mod, b_mod, norm1_g, w_in, conv_w, a_log, dt_bias, onorm_g, w_out,
           norm2_g, w_group, b_group, w_router, b_router, w_gate, w_up, w_down, final_g):
    b, t, d = x.shape
    tc = ctx.shape[1]
    rows = -(-(b + 1) // SUBLANES) * SUBLANES
    cc = jnp.zeros((rows, d), F32).at[:b].set(c).at[b].set(c_ctx)
    mod = _adaln(cc, w_mod, b_mod)
    mx = [mod[:b, i * d:(i + 1) * d].reshape(b, 1, d) for i in range(6)]
    mc = [jnp.broadcast_to(mod[b:b + 1, i * d:(i + 1) * d].reshape(1, 1, d), (b, 1, d)) for i in range(2)]

    _, qkv_c, _, gb_c, gt_c = _inproj(ctx.reshape(b * tc, d), mc[0], mc[1], norm1_g, w_in, a_log, dt_bias, tc)
    feat_c = _conv_features(qkv_c.reshape(b, tc, -1), conv_w, grid_mode=False)
    gb_c = gb_c.reshape(b, tc, LANES)
    zero_state = jnp.zeros((b, DN_HEADS, HEAD_W, HEAD_W), F32)
    _, s_fwd = _delta_scan(feat_c, gb_c, gt_c, zero_state, reverse=False)
    _, s_bwd = _delta_scan(feat_c, gb_c, gt_c, zero_state, reverse=True)

    x2 = x.reshape(b * t, d)
    f, qkv, z, gb, gt = _inproj(x2, mx[0], mx[1], norm1_g, w_in, a_log, dt_bias, t)
    feat = _conv_features(qkv.reshape(b, t, -1), conv_w, grid_mode=True)
    gb = gb.reshape(b, t, LANES)
    o_f, _ = _delta_scan(feat, gb, gt, s_fwd, reverse=False)
    o_b, _ = _delta_scan(feat, gb, gt, s_bwd, reverse=True)
    fo = _fourier_mix(f.reshape(b, t, -1))
    x1, h2, rt = _outproj(fo.reshape(b * t, -1), o_f.reshape(b * t, -1), o_b.reshape(b * t, -1), z, x2,
                          mx[2], mx[3], mx[4], onorm_g, norm2_g, w_out, w_group, b_group, w_router,
                          b_router, t)
    out = _moe_and_final(h2, rt, x1, mx[5], final_g, w_gate, w_up, w_down, t)
    return out.reshape(b, t, d)


def kernel(x, c, ctx, c_ctx, w_mod, b_mod, norm1_g, w_in, conv_w, a_log, dt_bias, onorm_g, w_out, norm2_g,
           w_group, b_group, w_router, b_router, w_gate, w_up, w_down, final_g):
    assert w_mod.shape[0] == 1, "single-layer trunk"
    return _layer(x, c, ctx, c_ctx, w_mod[0], b_mod[0], norm1_g[0], w_in[0], conv_w[0], a_log[0], dt_bias[0],
                  onorm_g[0], w_out[0], norm2_g[0], w_group[0], b_group[0], w_router[0], b_router[0],
                  w_gate[0], w_up[0], w_down[0], final_g)
```
